```python
import jax, jax.numpy as jnp
from jax import lax
import numpy as np

D_MODEL = 2048
BATCH = 16
SEQ = 2048
DEPTH = 1

HEAD_DIM = 128
A_HEADS = D_MODEL // (2 * HEAD_DIM)
A_WIDTH = A_HEADS * HEAD_DIM
DILATED_PAIRS = ((128, 1), (512, 4), (2048, 16))
BLK = 128
ROPE_THETA = 500000.0
A_ROT_DIM = HEAD_DIM // 4

B_HEADS = D_MODEL // (2 * HEAD_DIM)
MLA_Q_RANK = D_MODEL // 4
MLA_KV_RANK = D_MODEL // 4
MLA_NOPE = 128
MLA_ROPE = 64
MLA_V = 128
B_WIDTH = B_HEADS * MLA_V

MIX_WIDTH = A_WIDTH + B_WIDTH
IN_SPLITS = [A_WIDTH, 2 * A_WIDTH, 3 * A_WIDTH, 3 * A_WIDTH + MLA_Q_RANK,
             3 * A_WIDTH + MLA_Q_RANK + MLA_KV_RANK]
IN_COLS = 3 * A_WIDTH + MLA_Q_RANK + MLA_KV_RANK + MLA_ROPE

PEER_HEADS = 8
PEER_N_KEYS = 128
PEER_N_EXPERTS = PEER_N_KEYS ** 2
PEER_TOPK = 16
PEER_KEY_DIM = 256
PEER_CHUNK = 128

DN_ALPHA = (2 * DEPTH) ** 0.25
DN_BETA = (8 * DEPTH) ** -0.25

kernel_name = 'hybrid_dilated_mla_peer_block'


def layer_norm(x, g=None, b=None, eps=1e-5):
    xf = x.astype(jnp.float32)
    mu = xf.mean(-1, keepdims=True)
    var = jnp.square(xf - mu).mean(-1, keepdims=True)
    y = (xf - mu) * lax.rsqrt(var + eps)
    if g is not None:
        y = y * g.astype(jnp.float32) + b.astype(jnp.float32)
    return y.astype(x.dtype)


def rms_norm(x, g, eps=1e-6):
    xf = x.astype(jnp.float32)
    y = xf * lax.rsqrt(jnp.mean(xf * xf, -1, keepdims=True) + eps) * g.astype(jnp.float32)
    return y.astype(x.dtype)


def rope(x, pos, rot_dim):
    half = rot_dim // 2
    inv = ROPE_THETA ** (-jnp.arange(half, dtype=jnp.float32) * 2.0 / rot_dim)
    ang = pos[:, None] * inv[None, :]
    cos = jnp.cos(ang)[:, None, :]
    sin = jnp.sin(ang)[:, None, :]
    xf = x.astype(jnp.float32)
    x1 = xf[..., :half]
    x2 = xf[..., half:rot_dim]
    out = jnp.concatenate([x1 * cos - x2 * sin, x2 * cos + x1 * sin, xf[..., rot_dim:]], -1)
    return out.astype(x.dtype)


def dilated_window_attn(q, k, v, n_back, dil):
    B, S, H, E = q.shape
    L = S // dil
    nb = -(-L // BLK)
    Lp = nb * BLK

    def blocks(t):
        t = t.reshape(B, L, dil, H, E)
        t = jnp.pad(t, ((0, 0), (0, Lp - L), (0, 0), (0, 0), (0, 0)))
        return t.reshape(B, nb, BLK, dil, H, E)

    def with_prev(t):
        prev = jnp.pad(t, ((0, 0), (1, 0), (0, 0), (0, 0), (0, 0), (0, 0)))[:, :-1]
        return jnp.concatenate([prev, t], axis=2)

    qb = blocks(q)
    kk = with_prev(blocks(k))
    vv = with_prev(blocks(v))
    s = jnp.einsum('bnqrhe,bnkrhe->bnrhqk', qb, kk).astype(jnp.float32) * (E ** -0.5)
    qpos = jnp.arange(nb)[:, None] * BLK + jnp.arange(BLK)[None, :]
    kpos = jnp.arange(nb)[:, None] * BLK - BLK + jnp.arange(2 * BLK)[None, :]
    dist = qpos[:, :, None] - kpos[:, None, :]
    valid = (dist >= 0) & (dist <= n_back) & (kpos[:, None, :] >= 0)
    s = jnp.where(valid[None, :, None, None], s, -jnp.inf)
    m = s.max(-1, keepdims=True)
    p = jnp.exp(s - m)
    l = p.sum(-1, keepdims=True)
    o = jnp.einsum('bnrhqk,bnkrhe->bnqrhe', p.astype(vv.dtype), vv).astype(jnp.float32)
    l_t = jnp.moveaxis(l[..., 0], -1, 2)
    lse = jnp.moveaxis((m + jnp.log(l))[..., 0], -1, 2)
    o = o / l_t[..., None]
    o = o.reshape(B, Lp, dil, H, E)[:, :L].reshape(B, S, H, E)
    lse = lse.reshape(B, Lp, dil, H)[:, :L].reshape(B, S, H)
    return o, lse


def dilated_mixture(q, k, v):
    outs, lses = [], []
    for window, dil in DILATED_PAIRS:
        o, lse = dilated_window_attn(q, k, v, window // dil, dil)
        outs.append(o)
        lses.append(lse)
    wts = jax.nn.softmax(jnp.stack(lses, 0), axis=0)
    o = jnp.sum(wts[..., None] * jnp.stack(outs, 0), axis=0)
    return o.astype(q.dtype)


def mla_attention(q_nope, q_rope, k_nope, k_rope, v):
    B, S, H, _ = q_nope.shape
    nq = S // BLK
    scale = (MLA_NOPE + MLA_ROPE) ** -0.5
    kpos = jnp.arange(S)

    def to_blocks(t):
        return jnp.moveaxis(t.reshape(B, nq, BLK, *t.shape[2:]), 1, 0)

    def one_block(args):
        qn, qr, start = args
        s = (jnp.einsum('bqhe,bkhe->bhqk', qn, k_nope).astype(jnp.float32)
             + jnp.einsum('bqhr,bkr->bhqk', qr, k_rope).astype(jnp.float32)) * scale
        qpos = start + jnp.arange(BLK)
        s = jnp.where(kpos[None, :] <= qpos[:, None], s, -jnp.inf)
        p = jax.nn.softmax(s, axis=-1).astype(v.dtype)
        return jnp.einsum('bhqk,bkhe->bqhe', p, v)

    out = lax.map(one_block, (to_blocks(q_nope), to_blocks(q_rope), jnp.arange(nq) * BLK))
    return jnp.moveaxis(out, 0, 1).reshape(B, S, H, MLA_V)


def token_mix(h, pos, w_in, g_q_lat, g_kv_lat, w_uq, w_uk, w_uv, g_out_a, g_out_b, w_o):
    B, S, _ = h.shape
    proj = jnp.dot(h, w_in)
    qa, ka, va, cq, ckv, kr = jnp.split(proj, IN_SPLITS, axis=-1)
    qa = rope(qa.reshape(B, S, A_HEADS, HEAD_DIM), pos, A_ROT_DIM)
    ka = rope(ka.reshape(B, S, A_HEADS, HEAD_DIM), pos, A_ROT_DIM)
    va = va.reshape(B, S, A_HEADS, HEAD_DIM)
    o_a = dilated_mixture(qa, ka, va)
    cq = rms_norm(cq, g_q_lat)
    ckv = rms_norm(ckv, g_kv_lat)
    qb = jnp.dot(cq, w_uq).reshape(B, S, B_HEADS, MLA_NOPE + MLA_ROPE)
    q_nope = qb[..., :MLA_NOPE]
    q_rope = rope(qb[..., MLA_NOPE:], pos, MLA_ROPE)
    k_nope = jnp.dot(ckv, w_uk).reshape(B, S, B_HEADS, MLA_NOPE)
    v_b = jnp.dot(ckv, w_uv).reshape(B, S, B_HEADS, MLA_V)
    k_rope = rope(kr[:, :, None, :], pos, MLA_ROPE)[:, :, 0, :]
    o_b = mla_attention(q_nope, q_rope, k_nope, k_rope, v_b)
    o = jnp.concatenate([rms_norm(o_a.reshape(B, S, A_WIDTH), g_out_a),
                         rms_norm(o_b.reshape(B, S, B_WIDTH), g_out_b)], axis=-1)
    return jnp.dot(o, w_o)


def peer(h, w_pq, sub_key_1, sub_key_2, u_table, v_table):
    B, S, D = h.shape
    q = jnp.dot(h, w_pq).reshape(B, S, PEER_HEADS, 2, PEER_KEY_DIM // 2)
    s1 = jnp.einsum('bshe,ke->bshk', q[..., 0, :], sub_key_1).astype(jnp.float32)
    s2 = jnp.einsum('bshe,ke->bshk', q[..., 1, :], sub_key_2).astype(jnp.float32)
    v1, i1 = lax.top_k(s1, PEER_TOPK)
    v2, i2 = lax.top_k(s2, PEER_TOPK)
    cand = (v1[..., :, None] + v2[..., None, :]).reshape(B, S, PEER_HEADS, PEER_TOPK * PEER_TOPK)
    vals, ci = lax.top_k(cand, PEER_TOPK)
    idx = (jnp.take_along_axis(i1, ci // PEER_TOPK, -1) * PEER_N_KEYS
           + jnp.take_along_axis(i2, ci % PEER_TOPK, -1))
    g = jax.nn.softmax(vals, axis=-1)
    nc = (B * S) // PEER_CHUNK
    kk = PEER_HEADS * PEER_TOPK
    hc = h.reshape(nc, PEER_CHUNK, D)
    idc = idx.reshape(nc, PEER_CHUNK, kk)
    gc = g.reshape(nc, PEER_CHUNK, kk).astype(h.dtype)

    def chunk(args):
        ht, it, gt = args
        u = u_table[it]
        a = jnp.einsum('td,tkd->tk', ht, u)
        w = jax.nn.gelu(a.astype(jnp.float32), approximate=False).astype(ht.dtype) * gt
        return jnp.einsum('tk,tkd->td', w, v_table[it])

    return lax.map(chunk, (hc, idc, gc)).reshape(B, S, D)


def setup_inputs(seed: int = 0) -> dict:
    key = jax.random.key(seed)
    ks = jax.random.split(key, 24)

    def nrm(k, shape, s):
        return jax.random.normal(k, shape, jnp.float32) * s

    col_scale = jnp.ones((IN_COLS,), jnp.float32).at[2 * A_WIDTH:3 * A_WIDTH].set(DN_BETA)
    return {
        'x': nrm(ks[0], (BATCH, SEQ, D_MODEL), 1.0),
        'c': nrm(ks[1], (BATCH, D_MODEL), 1.0),
        'w_ada': nrm(ks[2], (DEPTH, D_MODEL, 6 * D_MODEL), 0.5 * D_MODEL ** -0.5),
        'b_ada': nrm(ks[3], (DEPTH, 6 * D_MODEL), 0.02),
        'w_in': nrm(ks[4], (DEPTH, D_MODEL, IN_COLS), D_MODEL ** -0.5) * col_scale,
        'g_q_lat': 1.0 + nrm(ks[5], (DEPTH, MLA_Q_RANK), 0.02),
        'g_kv_lat': 1.0 + nrm(ks[6], (DEPTH, MLA_KV_RANK), 0.02),
        'w_uq': nrm(ks[7], (DEPTH, MLA_Q_RANK, B_HEADS * (MLA_NOPE + MLA_ROPE)), MLA_Q_RANK ** -0.5),
        'w_uk': nrm(ks[8], (DEPTH, MLA_KV_RANK, B_HEADS * MLA_NOPE), MLA_KV_RANK ** -0.5),
        'w_uv': nrm(ks[9], (DEPTH, MLA_KV_RANK, B_HEADS * MLA_V), MLA_KV_RANK ** -0.5 * DN_BETA),
        'g_out_a': 1.0 + nrm(ks[10], (DEPTH, A_WIDTH), 0.02),
        'g_out_b': 1.0 + nrm(ks[11], (DEPTH, B_WIDTH), 0.02),
        'w_o': nrm(ks[12], (DEPTH, MIX_WIDTH, D_MODEL), MIX_WIDTH ** -0.5 * DN_BETA),
        'ln1_g': 1.0 + nrm(ks[13], (DEPTH, D_MODEL), 0.02),
        'ln1_b': nrm(ks[14], (DEPTH, D_MODEL), 0.02),
        'w_pq': nrm(ks[15], (DEPTH, D_MODEL, PEER_HEADS * PEER_KEY_DIM), D_MODEL ** -0.5),
        'sub_key_1': nrm(ks[16], (DEPTH, PEER_N_KEYS, PEER_KEY_DIM // 2), (PEER_KEY_DIM // 2) ** -0.5),
        'sub_key_2': nrm(ks[17], (DEPTH, PEER_N_KEYS, PEER_KEY_DIM // 2), (PEER_KEY_DIM // 2) ** -0.5),
        'u_table': nrm(ks[18], (DEPTH, PEER_N_EXPERTS, D_MODEL), D_MODEL ** -0.5),
        'v_table': nrm(ks[19], (DEPTH, PEER_N_EXPERTS, D_MODEL), DN_BETA),
        'ln2_g': 1.0 + nrm(ks[20], (DEPTH, D_MODEL), 0.02),
        'ln2_b': nrm(ks[21], (DEPTH, D_MODEL), 0.02),
    }


def reference(x, c, w_ada, b_ada, w_in, g_q_lat, g_kv_lat, w_uq, w_uk, w_uv, g_out_a, g_out_b,
              w_o, ln1_g, ln1_b, w_pq, sub_key_1, sub_key_2, u_table, v_table, ln2_g, ln2_b):
    S = x.shape[1]
    pos = jnp.arange(S, dtype=jnp.float32)
    c_act = jax.nn.silu(c)
    for l in range(DEPTH):
        mod = jnp.dot(c_act, w_ada[l]) + b_ada[l]
        sh1, sc1, g1, sh2, sc2, g2 = [m[:, None, :] for m in jnp.split(mod, 6, axis=-1)]
        h = layer_norm(x) * (1.0 + sc1) + sh1
        mix = token_mix(h, pos, w_in[l], g_q_lat[l], g_kv_lat[l], w_uq[l], w_uk[l], w_uv[l],
                        g_out_a[l], g_out_b[l], w_o[l])
        x = layer_norm(DN_ALPHA * x + g1 * mix, ln1_g[l], ln1_b[l])
        h = layer_norm(x) * (1.0 + sc2) + sh2
        ffn = peer(h, w_pq[l], sub_key_1[l], sub_key_2[l], u_table[l], v_table[l])
        x = layer_norm(DN_ALPHA * x + g2 * ffn, ln2_g[l], ln2_b[l])
    return x
```

```python
import functools

import jax
import jax.numpy as jnp
from jax import lax
from jax.experimental import pallas as pl
from jax.experimental.pallas import tpu as pltpu

F32 = jnp.float32
BF16 = jnp.bfloat16

D_MODEL = 2048
HEAD_DIM = 128
A_HEADS = 8
A_WIDTH = A_HEADS * HEAD_DIM
A_ROT_DIM = 32
ROPE_THETA = 500000.0
B_HEADS = 8
MLA_RANK = 512
MLA_NOPE = 128
MLA_ROPE = 64
MLA_V = 128
B_WIDTH = B_HEADS * MLA_V
MLA_QK_PAD = 256
PEER_HEADS = 8
PEER_N_KEYS = 128
PEER_N_EXPERTS = PEER_N_KEYS ** 2
PEER_TOPK = 16
DN_ALPHA = 2.0 ** 0.25
N_STATS = 4

V7X_LANES = 128
V7X_VMEM_LIMIT = 56 * 1024 * 1024

NEG_BIG = -1e30

_NT = (((1,), (1,)), ((), ()))


def _params(*sem):
    return pltpu.CompilerParams(dimension_semantics=sem, vmem_limit_bytes=V7X_VMEM_LIMIT)


def _resident(shape):
    nd = len(shape)
    return pl.BlockSpec(shape, lambda *_: (0,) * nd, pipeline_mode=pl.Buffered(1))


def _layer_norm(x, eps=1e-5):
    mu = jnp.mean(x, axis=-1, keepdims=True)
    xc = x - mu
    var = jnp.mean(xc * xc, axis=-1, keepdims=True)
    return xc * lax.rsqrt(var + eps)


def _rms_norm(x, g, eps=1e-6):
    return x * lax.rsqrt(jnp.mean(x * x, axis=-1, keepdims=True) + eps) * g


def _rope_lanes(x, cos, sin_lo, sin_hi, half):
    return (x * cos + pltpu.roll(x, V7X_LANES - half, 1) * sin_lo + pltpu.roll(x, half, 1) * sin_hi)


def _ada_kernel(c_ref, w_ref, b_ref, o_ref):
    c = c_ref[...]
    act = (c / (1.0 + jnp.exp(-c))).astype(BF16)
    o_ref[...] = jnp.dot(act, w_ref[...].astype(BF16), preferred_element_type=F32) + b_ref[...]


def _ada_mod(c, w_ada, b_ada):
    bsz, d = c.shape
    n = w_ada.shape[1]
    tn = 1024
    return pl.pallas_call(
        _ada_kernel,
        grid=(n // tn,),
        in_specs=[pl.BlockSpec((bsz, d), lambda j: (0, 0)),
                  pl.BlockSpec((d, tn), lambda j: (0, j)),
                  pl.BlockSpec((1, tn), lambda j: (0, j))],
        out_specs=pl.BlockSpec((bsz, tn), lambda j: (0, j)),
        out_shape=jax.ShapeDtypeStruct((bsz, n), F32),
        compiler_params=_params("arbitrary"),
        name="ada_mod",
    )(c, w_ada, b_ada.reshape(1, n))


def _in_proj_kernel(x_ref, sc_ref, sh_ref, w_ref, wuq_ref, wuk_ref, wuv_ref, gq_ref, gkv_ref,
                    ca_ref, sal_ref, sah_ref, cb_ref, sbl_ref, sbh_ref,
                    qa_ref, ka_ref, va_ref, qb_ref, kb_ref, vb_ref, *, scale_a, scale_b):
    h = (_layer_norm(x_ref[...]) * (1.0 + sc_ref[0]) + sh_ref[0]).astype(BF16)

    def proj(c0, width):
        return jnp.dot(h, w_ref[:, c0:c0 + width], preferred_element_type=F32)

    ca, sal, sah = ca_ref[...], sal_ref[...], sah_ref[...]
    cb, sbl, sbh = cb_ref[...], sbl_ref[...], sbh_ref[...]
    chunk = 512
    for which, (out_ref, scale) in enumerate(((qa_ref, scale_a), (ka_ref, None))):
        for cc in range(A_WIDTH // chunk):
            acc = proj(which * A_WIDTH + cc * chunk, chunk)
            for hh in range(chunk // HEAD_DIM):
                r = _rope_lanes(acc[:, hh * HEAD_DIM:(hh + 1) * HEAD_DIM], ca, sal, sah, A_ROT_DIM // 2)
                if scale is not None:
                    r = r * scale
                c0 = cc * chunk + hh * HEAD_DIM
                out_ref[:, c0:c0 + HEAD_DIM] = r.astype(BF16)
    for cc in range(A_WIDTH // chunk):
        va_ref[:, cc * chunk:(cc + 1) * chunk] = proj(2 * A_WIDTH + cc * chunk, chunk).astype(BF16)
    cq = _rms_norm(proj(3 * A_WIDTH, MLA_RANK), gq_ref[...]).astype(BF16)
    ckv = _rms_norm(proj(3 * A_WIDTH + MLA_RANK, MLA_RANK), gkv_ref[...]).astype(BF16)
    kr = _rope_lanes(proj(3 * A_WIDTH + 2 * MLA_RANK, V7X_LANES), cb, sbl, sbh, MLA_ROPE // 2).astype(BF16)
    for cc in range(B_HEADS * MLA_QK_PAD // chunk):
        qq = jnp.dot(cq, wuq_ref[:, cc * chunk:(cc + 1) * chunk], preferred_element_type=F32)
        for hh in range(chunk // MLA_QK_PAD):
            c0 = cc * chunk + hh * MLA_QK_PAD
            nope = qq[:, hh * MLA_QK_PAD:hh * MLA_QK_PAD + MLA_NOPE]
            rp = qq[:, hh * MLA_QK_PAD + MLA_NOPE:(hh + 1) * MLA_QK_PAD]
            qb_ref[:, c0:c0 + MLA_NOPE] = (nope * scale_b).astype(BF16)
            qb_ref[:, c0 + MLA_NOPE:c0 + MLA_QK_PAD] = (
                _rope_lanes(rp, cb, sbl, sbh, MLA_ROPE // 2) * scale_b).astype(BF16)
    for cc in range(B_WIDTH // chunk):
        kn = jnp.dot(ckv, wuk_ref[:, cc * chunk:(cc + 1) * chunk], preferred_element_type=F32)
        for hh in range(chunk // MLA_NOPE):
            head = cc * (chunk // MLA_NOPE) + hh
            kb_ref[:, head * MLA_QK_PAD:head * MLA_QK_PAD + MLA_NOPE] = (
                kn[:, hh * MLA_NOPE:(hh + 1) * MLA_NOPE].astype(BF16))
            kb_ref[:, head * MLA_QK_PAD + MLA_NOPE:(head + 1) * MLA_QK_PAD] = kr
        vb_ref[:, cc * chunk:(cc + 1) * chunk] = jnp.dot(
            ckv, wuv_ref[:, cc * chunk:(cc + 1) * chunk], preferred_element_type=F32).astype(BF16)


def _in_proj(x2, mod3, w_in_p, wuq_r, wuk, wuv, gq, gkv, tabs_a, tabs_b, seq):
    t, d = x2.shape
    tm = 256
    per_b = seq // tm
    row = lambda i: (i, 0)
    tab = lambda i: (i % per_b, 0)
    mod_spec = lambda k: pl.BlockSpec((1, 1, d), lambda i: ((i // per_b) * 6 + k, 0, 0))
    outs = [(A_WIDTH, BF16), (A_WIDTH, BF16), (A_WIDTH, BF16),
            (B_HEADS * MLA_QK_PAD, BF16), (B_HEADS * MLA_QK_PAD, BF16), (B_WIDTH, BF16)]
    return pl.pallas_call(
        functools.partial(_in_proj_kernel, scale_a=HEAD_DIM ** -0.5, scale_b=(MLA_NOPE + MLA_ROPE) ** -0.5),
        grid=(t // tm,),
        in_specs=[pl.BlockSpec((tm, d), row), mod_spec(1), mod_spec(0),
                  _resident(w_in_p.shape), _resident(wuq_r.shape), _resident(wuk.shape), _resident(wuv.shape),
                  _resident(gq.shape), _resident(gkv.shape)]
                 + [pl.BlockSpec((tm, V7X_LANES), tab)] * 6,
        out_specs=[pl.BlockSpec((tm, w), row) for w, _ in outs],
        out_shape=[jax.ShapeDtypeStruct((t, w), dt) for w, dt in outs],
        compiler_params=_params("arbitrary"),
        name="in_proj",
    )(x2, mod3, mod3, w_in_p, wuq_r, wuk, wuv, gq, gkv, *tabs_a, *tabs_b)


def _attn_kernel(q_ref, k_ref, v_ref, bias_ref, o_ref, m_sc, l_sc, acc_sc, *, tq, tk, bias_everywhere):
    qi = pl.program_id(2)
    ratio = tq // tk
    q = q_ref[...]
    m_sc[...] = jnp.full(m_sc.shape, NEG_BIG, F32)
    l_sc[...] = jnp.zeros(l_sc.shape, F32)
    acc_sc[...] = jnp.zeros(acc_sc.shape, F32)

    def step(j, bias_idx):
        k0 = pl.multiple_of(j * tk, tk)
        s = lax.dot_general(q, k_ref[pl.ds(k0, tk), :], _NT, preferred_element_type=F32)
        if bias_idx is not None:
            s = s + bias_ref[bias_idx]
        m_prev = m_sc[...]
        m_new = jnp.maximum(m_prev, jnp.max(s, axis=-1, keepdims=True))
        alpha = jnp.exp(m_prev - m_new)
        p = jnp.exp(s - m_new)
        l_sc[...] = alpha * l_sc[...] + jnp.sum(p, axis=-1, keepdims=True)
        acc_sc[...] = alpha * acc_sc[...] + jnp.dot(p.astype(BF16), v_ref[pl.ds(k0, tk), :],
                                                    preferred_element_type=F32)
        m_sc[...] = m_new

    n_diag = ratio
    n_full = qi * ratio
    if bias_everywhere:
        def body(j, carry):
            step(j, n_full + 1 - j)
            return carry
        lax.fori_loop(0, n_full + n_diag, body, 0)
    else:
        def body(j, carry):
            step(j, None)
            return carry
        lax.fori_loop(0, n_full, body, 0)
        for dj in range(n_diag):
            step(n_full + dj, 1 - dj)
    o_ref[...] = acc_sc[...] / l_sc[...]


def _attention(q, k, v, bias, *, seq, heads, ek, bias_everywhere):
    t = q.shape[0]
    bsz = t // seq
    tq, tk = bias.shape[1], bias.shape[2]
    nq = seq // tq
    ev = v.shape[1] // heads
    return pl.pallas_call(
        functools.partial(_attn_kernel, tq=tq, tk=tk, bias_everywhere=bias_everywhere),
        grid=(bsz, heads, nq),
        in_specs=[pl.BlockSpec((tq, ek), lambda b, h, i: (b * nq + i, h)),
                  pl.BlockSpec((seq, ek), lambda b, h, i: (b, h)),
                  pl.BlockSpec((seq, ev), lambda b, h, i: (b, h)),
                  _resident(bias.shape)],
        out_specs=pl.BlockSpec((tq, ev), lambda b, h, i: (b * nq + i, h)),
        out_shape=jax.ShapeDtypeStruct((t, heads * ev), F32),
        scratch_shapes=[pltpu.VMEM((tq, 1), F32), pltpu.VMEM((tq, 1), F32), pltpu.VMEM((tq, ev), F32)],
        compiler_params=_params("arbitrary", "arbitrary", "arbitrary"),
        name="attention",
    )(q, k, v, bias)


def _bias_tables(tq, tk, seq):
    n_off = seq // tk
    off = (jnp.arange(n_off, dtype=jnp.int32) - 1) * tk
    d = off[:, None, None] + jnp.arange(tq, dtype=jnp.int32)[None, :, None] \
        - jnp.arange(tk, dtype=jnp.int32)[None, None, :]
    ok = d >= 0
    mult = ((d <= 128).astype(F32) + ((d % 4 == 0) & (d <= 512)).astype(F32)
            + ((d % 16 == 0) & (d <= 2048)).astype(F32))
    mix = jnp.where(ok & (mult > 0), jnp.log(jnp.maximum(mult, 1.0)), NEG_BIG).astype(F32)
    causal = jnp.where(ok[:2], 0.0, NEG_BIG).astype(F32)
    return mix, causal


def _post_attn_kernel(oa_ref, ob_ref, x_ref, g1_ref, sc_ref, sh_ref, ga_ref, gb_ref, wo_ref, l1g_ref, l1b_ref,
                      wpq_ref, sk_ref, x1_ref, h2_ref, st_ref):
    na = _rms_norm(oa_ref[...], ga_ref[...]).astype(BF16)
    nb = _rms_norm(ob_ref[...], gb_ref[...]).astype(BF16)
    mix = (jnp.dot(na, wo_ref[0:A_WIDTH, :], preferred_element_type=F32)
           + jnp.dot(nb, wo_ref[A_WIDTH:A_WIDTH + B_WIDTH, :], preferred_element_type=F32))
    x1 = _layer_norm(DN_ALPHA * x_ref[...] + g1_ref[0] * mix) * l1g_ref[...] + l1b_ref[...]
    x1_ref[...] = x1
    h2 = (_layer_norm(x1) * (1.0 + sc_ref[0]) + sh_ref[0]).astype(BF16)
    h2_ref[...] = h2
    q = jnp.dot(h2, wpq_ref[...], preferred_element_type=F32).astype(BF16)
    for hh in range(PEER_HEADS):
        for half in range(2):
            r0 = (hh * 2 + half) * PEER_N_KEYS
            st_ref[r0:r0 + PEER_N_KEYS, :] = lax.dot_general(
                sk_ref[half], q[:, r0:r0 + PEER_N_KEYS], _NT, preferred_element_type=F32)


def _post_attn(oa, ob, x2, mod3, ga, gb, wo, l1g, l1b, wpq, sk, seq):
    t, d = x2.shape
    tm = 256
    per_b = seq // tm
    row = lambda i: (i, 0)
    mod_spec = lambda k: pl.BlockSpec((1, 1, d), lambda i: ((i // per_b) * 6 + k, 0, 0))
    return pl.pallas_call(
        _post_attn_kernel,
        grid=(t // tm,),
        in_specs=[pl.BlockSpec((tm, A_WIDTH), row), pl.BlockSpec((tm, B_WIDTH), row), pl.BlockSpec((tm, d), row),
                  mod_spec(2), mod_spec(4), mod_spec(3),
                  _resident(ga.shape), _resident(gb.shape), _resident(wo.shape),
                  _resident(l1g.shape), _resident(l1b.shape), _resident(wpq.shape), _resident(sk.shape)],
        out_specs=[pl.BlockSpec((tm, d), row), pl.BlockSpec((tm, d), row),
                   pl.BlockSpec((2 * PEER_HEADS * PEER_N_KEYS, tm), lambda i: (0, i))],
        out_shape=[jax.ShapeDtypeStruct((t, d), F32), jax.ShapeDtypeStruct((t, d), BF16),
                   jax.ShapeDtypeStruct((2 * PEER_HEADS * PEER_N_KEYS, t), F32)],
        compiler_params=_params("arbitrary"),
        name="post_attn",
    )(oa, ob, x2, mod3, mod3, mod3, ga, gb, wo, l1g, l1b, wpq, sk)


def _top16_rows(s):
    rows = []
    for _ in range(PEER_TOPK):
        m = jnp.max(s, axis=0, keepdims=True)
        rows.append(m)
        s = jnp.where(s == m, -jnp.inf, s)
    return rows


def _peer_topk_kernel(st_ref, stats_ref):
    tl = st_ref.shape[1]
    sub = lax.broadcasted_iota(jnp.int32, (8, tl), 0)
    for hh in range(PEER_HEADS):
        r0 = hh * 2 * PEER_N_KEYS
        v1 = _top16_rows(st_ref[r0:r0 + PEER_N_KEYS, :])
        v2 = _top16_rows(st_ref[r0 + PEER_N_KEYS:r0 + 2 * PEER_N_KEYS, :])
        v2_lo = jnp.concatenate(v2[:8], axis=0)
        v2_hi = jnp.concatenate(v2[8:], axis=0)
        blocks = [v1[0] + v2_lo, v1[0] + v2_hi, v1[1] + v2_lo]
        for a in range(2, 8):
            blocks.append(jnp.where(sub < PEER_TOPK // (a + 1), v1[a] + v2_lo, -jnp.inf))
        blocks.append(jnp.concatenate(v1[8:], axis=0) + v2[0])
        cand = jnp.concatenate(blocks, axis=0)
        cmax = v1[0] + v2[0]
        z = jnp.zeros_like(cmax)
        tau = cmax
        for _ in range(PEER_TOPK):
            tau = jnp.max(cand, axis=0, keepdims=True)
            z = z + jnp.exp(tau - cmax)
            cand = jnp.where(cand == tau, -jnp.inf, cand)
        s0 = hh * N_STATS
        stats_ref[s0:s0 + 1, :] = tau
        stats_ref[s0 + 1:s0 + 2, :] = v1[0]
        stats_ref[s0 + 2:s0 + 3, :] = v2[0]
        stats_ref[s0 + 3:s0 + 4, :] = 1.0 / z


def _peer_topk(st):
    rows, t = st.shape
    tl = 256
    return pl.pallas_call(
        _peer_topk_kernel,
        grid=(t // tl,),
        in_specs=[pl.BlockSpec((rows, tl), lambda i: (0, i))],
        out_specs=pl.BlockSpec((PEER_HEADS * N_STATS, tl), lambda i: (0, i)),
        out_shape=jax.ShapeDtypeStruct((PEER_HEADS * N_STATS, t), F32),
        compiler_params=_params("arbitrary"),
        name="peer_topk",
    )(st)


def _gelu(x):
    return 0.5 * x * (1.0 + lax.erf(x * (2.0 ** -0.5)))


def _peer_dense_kernel(h2_ref, st_ref, stats_ref, u_ref, vt_ref, o_ref, acc_sc, e2_sc, *, te):
    e = pl.program_id(1)
    n_i1 = te // PEER_N_KEYS

    def s2_rows(hh):
        r0 = (hh * 2 + 1) * PEER_N_KEYS
        return st_ref[r0:r0 + PEER_N_KEYS, :]

    @pl.when(e == 0)
    def _():
        acc_sc[...] = jnp.zeros(acc_sc.shape, F32)
        for hh in range(PEER_HEADS):
            m2 = stats_ref[hh * N_STATS + 2:hh * N_STATS + 3, :]
            e2_sc[hh * PEER_N_KEYS:(hh + 1) * PEER_N_KEYS, :] = jnp.exp(s2_rows(hh) - m2)

    a_t = lax.dot_general(u_ref[...], h2_ref[...], _NT, preferred_element_type=F32)
    parts = []
    for ii in range(n_i1):
        i1 = e * n_i1 + ii
        w = jnp.zeros((PEER_N_KEYS, a_t.shape[1]), F32)
        for hh in range(PEER_HEADS):
            s0 = hh * N_STATS
            s1row = st_ref[pl.ds(hh * 2 * PEER_N_KEYS + i1, 1), :]
            e1 = jnp.exp(s1row - stats_ref[s0 + 1:s0 + 2, :]) * stats_ref[s0 + 3:s0 + 4, :]
            sel = (s2_rows(hh) + s1row) >= stats_ref[s0:s0 + 1, :]
            w = w + jnp.where(sel, e2_sc[hh * PEER_N_KEYS:(hh + 1) * PEER_N_KEYS, :] * e1, 0.0)
        parts.append((_gelu(a_t[ii * PEER_N_KEYS:(ii + 1) * PEER_N_KEYS, :]) * w).astype(BF16))
    g_t = jnp.concatenate(parts, axis=0)
    acc_sc[...] += jnp.dot(vt_ref[...], g_t, preferred_element_type=F32)

    @pl.when(e == pl.num_programs(1) - 1)
    def _():
        o_ref[...] = acc_sc[...].T


def _peer_dense(h2, st, stats, u_bf, vt_bf):
    t, d = h2.shape
    n_e = u_bf.shape[0]
    tm, te = 512, 512
    return pl.pallas_call(
        functools.partial(_peer_dense_kernel, te=te),
        grid=(t // tm, n_e // te),
        in_specs=[pl.BlockSpec((tm, d), lambda i, e: (i, 0)),
                  pl.BlockSpec((st.shape[0], tm), lambda i, e: (0, i)),
                  pl.BlockSpec((stats.shape[0], tm), lambda i, e: (0, i)),
                  pl.BlockSpec((te, d), lambda i, e: (e, 0)),
                  pl.BlockSpec((d, te), lambda i, e: (0, e))],
        out_specs=pl.BlockSpec((tm, d), lambda i, e: (i, 0)),
        out_shape=jax.ShapeDtypeStruct((t, d), F32),
        scratch_shapes=[pltpu.VMEM((d, tm), F32), pltpu.VMEM((PEER_HEADS * PEER_N_KEYS, tm), F32)],
        compiler_params=_params("arbitrary", "arbitrary"),
        name="peer_dense",
    )(h2, st, stats, u_bf, vt_bf)


def _final_ln_kernel(x1_ref, f_ref, g2_ref, lg_ref, lb_ref, o_ref):
    o_ref[...] = _layer_norm(DN_ALPHA * x1_ref[...] + g2_ref[0] * f_ref[...]) * lg_ref[...] + lb_ref[...]


def _final_ln(x1, ffn, mod3, lg, lb, seq):
    t, d = x1.shape
    tm = 512
    per_b = seq // tm
    row = lambda i: (i, 0)
    return pl.pallas_call(
        _final_ln_kernel,
        grid=(t // tm,),
        in_specs=[pl.BlockSpec((tm, d), row), pl.BlockSpec((tm, d), row),
                  pl.BlockSpec((1, 1, d), lambda i: ((i // per_b) * 6 + 5, 0, 0)),
                  _resident(lg.shape), _resident(lb.shape)],
        out_specs=pl.BlockSpec((tm, d), row),
        out_shape=jax.ShapeDtypeStruct((t, d), F32),
        compiler_params=_params("arbitrary"),
        name="final_ln",
    )(x1, ffn, mod3, lg, lb)


def _rope_tables(seq, rot_dim):
    half = rot_dim // 2
    pos = jnp.arange(seq, dtype=F32)
    inv = ROPE_THETA ** (-jnp.arange(half, dtype=F32) * 2.0 / rot_dim)
    ang = pos[:, None] * inv[None, :]
    cos, sin = jnp.cos(ang), jnp.sin(ang)
    pad = V7X_LANES - rot_dim
    ones = jnp.ones((seq, pad), F32)
    zeros = jnp.zeros((seq, pad), F32)
    zh = jnp.zeros((seq, half), F32)
    return (jnp.concatenate([cos, cos, ones], 1),
            jnp.concatenate([-sin, zh, zeros], 1),
            jnp.concatenate([zh, sin, zeros], 1))


def _layer(x2, mod3, seq, w_in, g_q_lat, g_kv_lat, w_uq, w_uk, w_uv, g_out_a, g_out_b, w_o,
           ln1_g, ln1_b, w_pq, sub_key_1, sub_key_2, u_table, v_table, ln2_g, ln2_b):
    d = x2.shape[1]
    n_in = w_in.shape[1]
    w_in_p = jnp.pad(w_in.astype(BF16), ((0, 0), (0, (-n_in) % V7X_LANES)))
    pad_q = MLA_QK_PAD - MLA_NOPE - MLA_ROPE
    wuq_r = jnp.pad(w_uq.astype(BF16).reshape(MLA_RANK, B_HEADS, MLA_NOPE + MLA_ROPE),
                    ((0, 0), (0, 0), (0, pad_q))).reshape(MLA_RANK, B_HEADS * MLA_QK_PAD)
    tabs_a = _rope_tables(seq, A_ROT_DIM)
    tabs_b = _rope_tables(seq, MLA_ROPE)
    qa, ka, va, qb, kb, vb = _in_proj(
        x2, mod3, w_in_p, wuq_r, w_uk.astype(BF16), w_uv.astype(BF16),
        g_q_lat.reshape(1, -1), g_kv_lat.reshape(1, -1), tabs_a, tabs_b, seq)

    bias_mix, bias_causal = _bias_tables(512, 256, seq)
    oa = _attention(qa, ka, va, bias_mix, seq=seq, heads=A_HEADS, ek=HEAD_DIM, bias_everywhere=True)
    ob = _attention(qb, kb, vb, bias_causal, seq=seq, heads=B_HEADS, ek=MLA_QK_PAD, bias_everywhere=False)

    sk = jnp.stack([sub_key_1, sub_key_2]).astype(BF16)
    x1, h2, st = _post_attn(oa, ob, x2, mod3, g_out_a.reshape(1, -1), g_out_b.reshape(1, -1),
                            w_o.astype(BF16), ln1_g.reshape(1, -1), ln1_b.reshape(1, -1),
                            w_pq.astype(BF16), sk, seq)
    stats = _peer_topk(st)
    ffn = _peer_dense(h2, st, stats, u_table.astype(BF16), v_table.astype(BF16).T)
    return _final_ln(x1, ffn, mod3, ln2_g.reshape(1, d), ln2_b.reshape(1, d), seq)


def kernel(x, c, w_ada, b_ada, w_in, g_q_lat, g_kv_lat, w_uq, w_uk, w_uv, g_out_a, g_out_b, w_o, ln1_g, ln1_b,
           w_pq, sub_key_1, sub_key_2, u_table, v_table, ln2_g, ln2_b):
    bsz, seq, d = x.shape
    x2 = x.reshape(bsz * seq, d)
    for l in range(w_ada.shape[0]):
        mod3 = _ada_mod(c, w_ada[l], b_ada[l]).reshape(bsz * 6, 1, d)
        x2 = _layer(x2, mod3, seq, w_in[l], g_q_lat[l], g_kv_lat[l], w_uq[l], w_uk[l], w_uv[l],
                    g_out_a[l], g_out_b[l], w_o[l], ln1_g[l], ln1_b[l], w_pq[l], sub_key_1[l], sub_key_2[l],
                    u_table[l], v_table[l], ln2_g[l], ln2_b[l])
    return x2.reshape(bsz, seq, d)
```

```python
import functools
import math

import jax
import jax.numpy as jnp
from jax import lax
from jax.experimental import pallas as pl
from jax.experimental.pallas import tpu as pltpu

F32 = jnp.float32
BF16 = jnp.bfloat16

D_MODEL = 2048
HEAD_DIM = 128
A_HEADS = 8
A_WIDTH = A_HEADS * HEAD_DIM
A_ROT_DIM = 32
ROPE_THETA = 500000.0
B_HEADS = 8
MLA_RANK = 512
MLA_NOPE = 128
MLA_ROPE = 64
MLA_V = 128
B_WIDTH = B_HEADS * MLA_V
MLA_QK_PAD = 256
PEER_HEADS = 8
PEER_N_KEYS = 128
PEER_TOPK = 16
DN_ALPHA = 2.0 ** 0.25

V7X_LANES = 128
V7X_BF16_SUBLANES = 16
V7X_VMEM_LIMIT = 56 * 1024 * 1024

TM_PROJ = 256
TQ_ATTN = 256
TL_TOPK = 256
TM_PEER = 1024
TS_PEER = 512
TE_PEER = 1024
TM_FINAL = 512

NEG_BIG = -1e30
LOG2E = math.log2(math.e)

_NT = (((1,), (1,)), ((), ()))

_C_QA, _C_KA, _C_CQ, _C_CKV, _C_KR = 0, A_WIDTH, 2 * A_WIDTH, 2 * A_WIDTH + MLA_RANK, 2 * A_WIDTH + 2 * MLA_RANK


def _params(*sem):
    return pltpu.CompilerParams(dimension_semantics=sem, vmem_limit_bytes=V7X_VMEM_LIMIT)


def _resident(shape):
    nd = len(shape)
    return pl.BlockSpec(shape, lambda *_: (0,) * nd, pipeline_mode=pl.Buffered(1))


def _layer_norm(x, eps=1e-5):
    mu = jnp.mean(x, axis=-1, keepdims=True)
    xc = x - mu
    var = jnp.mean(xc * xc, axis=-1, keepdims=True)
    return xc * lax.rsqrt(var + eps)


def _rms_norm(x, g, eps=1e-6):
    return x * lax.rsqrt(jnp.mean(x * x, axis=-1, keepdims=True) + eps) * g


def _rope_lanes(x, cos, sin_lo, sin_hi, half):
    return (x * cos + pltpu.roll(x, V7X_LANES - half, 1) * sin_lo + pltpu.roll(x, half, 1) * sin_hi)


def _ada_kernel(c_ref, w_ref, b_ref, o_ref):
    c = c_ref[...]
    act = (c / (1.0 + jnp.exp(-c))).astype(BF16)
    o_ref[...] = jnp.dot(act, w_ref[...].astype(BF16), preferred_element_type=F32) + b_ref[...]


def _ada_mod(c, w_ada, b_ada):
    bsz, d = c.shape
    n = w_ada.shape[1]
    tn = 1024
    return pl.pallas_call(
        _ada_kernel,
        grid=(n // tn,),
        in_specs=[pl.BlockSpec((bsz, d), lambda j: (0, 0)),
                  pl.BlockSpec((d, tn), lambda j: (0, j)),
                  pl.BlockSpec((1, tn), lambda j: (0, j))],
        out_specs=pl.BlockSpec((bsz, tn), lambda j: (0, j)),
        out_shape=jax.ShapeDtypeStruct((bsz, n), F32),
        compiler_params=_params("arbitrary"),
        name="ada_mod",
    )(c, w_ada, b_ada.reshape(1, n))


def _in_proj_kernel(x_ref, sc_ref, sh_ref, w_ref, wvt_ref, wuq_ref, wuk_ref, wuvt_ref, gq_ref, gkv_ref,
                    ca_ref, sal_ref, sah_ref, cb_ref, sbl_ref, sbh_ref,
                    qa_ref, ka_ref, vat_ref, qb_ref, kb_ref, vbt_ref, *, scale_a, scale_b):
    h = (_layer_norm(x_ref[...]) * (1.0 + sc_ref[0]) + sh_ref[0]).astype(BF16)

    def proj(c0, width):
        return jnp.dot(h, w_ref[:, c0:c0 + width], preferred_element_type=F32)

    ca, sal, sah = ca_ref[...], sal_ref[...], sah_ref[...]
    cb, sbl, sbh = cb_ref[...], sbl_ref[...], sbh_ref[...]
    chunk = 512
    for c_base, out_ref, scale in ((_C_QA, qa_ref, scale_a), (_C_KA, ka_ref, None)):
        for cc in range(A_WIDTH // chunk):
            acc = proj(c_base + cc * chunk, chunk)
            for hh in range(chunk // HEAD_DIM):
                r = _rope_lanes(acc[:, hh * HEAD_DIM:(hh + 1) * HEAD_DIM], ca, sal, sah, A_ROT_DIM // 2)
                if scale is not None:
                    r = r * scale
                c0 = cc * chunk + hh * HEAD_DIM
                out_ref[:, c0:c0 + HEAD_DIM] = r.astype(BF16)
    vat_ref[...] = lax.dot_general(wvt_ref[...], h, _NT, preferred_element_type=F32).astype(BF16)
    cq = _rms_norm(proj(_C_CQ, MLA_RANK), gq_ref[...]).astype(BF16)
    ckv = _rms_norm(proj(_C_CKV, MLA_RANK), gkv_ref[...]).astype(BF16)
    kr = _rope_lanes(proj(_C_KR, V7X_LANES), cb, sbl, sbh, MLA_ROPE // 2).astype(BF16)
    for cc in range(B_HEADS * MLA_QK_PAD // chunk):
        qq = jnp.dot(cq, wuq_ref[:, cc * chunk:(cc + 1) * chunk], preferred_element_type=F32)
        for hh in range(chunk // MLA_QK_PAD):
            c0 = cc * chunk + hh * MLA_QK_PAD
            nope = qq[:, hh * MLA_QK_PAD:hh * MLA_QK_PAD + MLA_NOPE]
            rp = qq[:, hh * MLA_QK_PAD + MLA_NOPE:(hh + 1) * MLA_QK_PAD]
            qb_ref[:, c0:c0 + MLA_NOPE] = (nope * scale_b).astype(BF16)
            qb_ref[:, c0 + MLA_NOPE:c0 + MLA_QK_PAD] = (
                _rope_lanes(rp, cb, sbl, sbh, MLA_ROPE // 2) * scale_b).astype(BF16)
    for cc in range(B_WIDTH // chunk):
        kn = jnp.dot(ckv, wuk_ref[:, cc * chunk:(cc + 1) * chunk], preferred_element_type=F32)
        for hh in range(chunk // MLA_NOPE):
            head = cc * (chunk // MLA_NOPE) + hh
            kb_ref[:, head * MLA_QK_PAD:head * MLA_QK_PAD + MLA_NOPE] = (
                kn[:, hh * MLA_NOPE:(hh + 1) * MLA_NOPE].astype(BF16))
            kb_ref[:, head * MLA_QK_PAD + MLA_NOPE:(head + 1) * MLA_QK_PAD] = kr
    vbt_ref[...] = lax.dot_general(wuvt_ref[...], ckv, _NT, preferred_element_type=F32).astype(BF16)


def _in_proj(x2, mod3, w_main, wvt, wuq_r, wuk, wuvt, gq, gkv, tabs_a, tabs_b, seq):
    t, d = x2.shape
    tm = TM_PROJ
    per_b = seq // tm
    row = lambda i: (i, 0)
    col = lambda i: (0, i)
    tab = lambda i: (i % per_b, 0)
    mod_spec = lambda k: pl.BlockSpec((1, 1, d), lambda i: ((i // per_b) * 6 + k, 0, 0))
    qk_b = B_HEADS * MLA_QK_PAD
    out_specs = [pl.BlockSpec((tm, A_WIDTH), row), pl.BlockSpec((tm, A_WIDTH), row),
                 pl.BlockSpec((A_WIDTH, tm), col),
                 pl.BlockSpec((tm, qk_b), row), pl.BlockSpec((tm, qk_b), row),
                 pl.BlockSpec((B_WIDTH, tm), col)]
    out_shape = [jax.ShapeDtypeStruct(s, BF16) for s in
                 ((t, A_WIDTH), (t, A_WIDTH), (A_WIDTH, t), (t, qk_b), (t, qk_b), (B_WIDTH, t))]
    return pl.pallas_call(
        functools.partial(_in_proj_kernel, scale_a=HEAD_DIM ** -0.5 * LOG2E,
                          scale_b=(MLA_NOPE + MLA_ROPE) ** -0.5 * LOG2E),
        grid=(t // tm,),
        in_specs=[pl.BlockSpec((tm, d), row), mod_spec(1), mod_spec(0),
                  _resident(w_main.shape), _resident(wvt.shape), _resident(wuq_r.shape), _resident(wuk.shape),
                  _resident(wuvt.shape), _resident(gq.shape), _resident(gkv.shape)]
                 + [pl.BlockSpec((tm, V7X_LANES), tab)] * 6,
        out_specs=out_specs,
        out_shape=out_shape,
        compiler_params=_params("arbitrary"),
        name="in_proj",
    )(x2, mod3, mod3, w_main, wvt, wuq_r, wuk, wuvt, gq, gkv, *tabs_a, *tabs_b)


def _attn_kernel(q_ref, k_ref, vt_ref, bias_ref, o_ref, *, tq, bias_all_keys):
    seq = q_ref.shape[0]
    nq = seq // tq
    nb = bias_ref.shape[0]

    def key_parts(qi):
        n = (qi + 1) * tq
        if bias_all_keys:
            return [(0, n, True)]
        return ([(0, n - tq, False)] if qi > 0 else []) + [(n - tq, n, True)]

    def score_tiles(qi):
        q = q_ref[qi * tq:(qi + 1) * tq, :]
        tiles = []
        for k0, k1, biased in key_parts(qi):
            st = lax.dot_general(k_ref[k0:k1, :], q, _NT, preferred_element_type=F32)
            tiles.append(st + bias_ref[nb - (k1 - k0):nb, :] if biased else st)
        return tiles

    ahead = score_tiles(0)
    for qi in range(nq):
        parts, scores = key_parts(qi), ahead
        if qi + 1 < nq:
            ahead = score_tiles(qi + 1)
        m = jnp.max(scores[0], axis=0, keepdims=True)
        for st in scores[1:]:
            m = jnp.maximum(m, jnp.max(st, axis=0, keepdims=True))
        l = jnp.zeros_like(m)
        o_t = jnp.zeros((vt_ref.shape[0], tq), F32)
        for (k0, k1, _), st in zip(parts, scores):
            p = jnp.exp2(st - m)
            l = l + jnp.sum(p, axis=0, keepdims=True)
            o_t = o_t + jnp.dot(vt_ref[:, k0:k1], p.astype(BF16), preferred_element_type=F32)
        o_ref[qi * tq:(qi + 1) * tq, :] = (o_t * (1.0 / l)).T


def _attention(q, k, vt, bias, *, seq, heads, ek, bias_all_keys):
    t = q.shape[0]
    bsz = t // seq
    ev = vt.shape[0] // heads
    return pl.pallas_call(
        functools.partial(_attn_kernel, tq=TQ_ATTN, bias_all_keys=bias_all_keys),
        grid=(bsz, heads),
        in_specs=[pl.BlockSpec((seq, ek), lambda b, h: (b, h)),
                  pl.BlockSpec((seq, ek), lambda b, h: (b, h)),
                  pl.BlockSpec((ev, seq), lambda b, h: (h, b)),
                  _resident(bias.shape)],
        out_specs=pl.BlockSpec((seq, ev), lambda b, h: (b, h)),
        out_shape=jax.ShapeDtypeStruct((t, heads * ev), F32),
        compiler_params=_params("arbitrary", "arbitrary"),
        name="attention",
    )(q, k, vt, bias)


def _bias_tables(tq, seq):
    d = ((seq - tq) + jnp.arange(tq, dtype=jnp.int32)[None, :] - jnp.arange(seq, dtype=jnp.int32)[:, None])
    ok = d >= 0
    mult = ((d <= 128).astype(F32) + ((d % 4 == 0) & (d <= 512)).astype(F32)
            + ((d % 16 == 0) & (d <= 2048)).astype(F32))
    mix = jnp.where(ok & (mult > 0), jnp.log2(jnp.maximum(mult, 1.0)), NEG_BIG).astype(F32)
    causal = jnp.where(ok[seq - tq:], 0.0, NEG_BIG).astype(F32)
    return mix, causal


def _post_attn_kernel(oa_ref, ob_ref, x_ref, g1_ref, sc_ref, sh_ref, ga_ref, gb_ref, wo_ref, l1g_ref, l1b_ref,
                      wpq_ref, sk_ref, x1_ref, h2_ref, st_ref):
    na = _rms_norm(oa_ref[...], ga_ref[...]).astype(BF16)
    nb = _rms_norm(ob_ref[...], gb_ref[...]).astype(BF16)
    mix = (jnp.dot(na, wo_ref[0:A_WIDTH, :], preferred_element_type=F32)
           + jnp.dot(nb, wo_ref[A_WIDTH:A_WIDTH + B_WIDTH, :], preferred_element_type=F32))
    x1 = _layer_norm(DN_ALPHA * x_ref[...] + g1_ref[0] * mix) * l1g_ref[...] + l1b_ref[...]
    x1_ref[...] = x1
    h2 = (_layer_norm(x1) * (1.0 + sc_ref[0]) + sh_ref[0]).astype(BF16)
    h2_ref[...] = h2
    q = jnp.dot(h2, wpq_ref[...], preferred_element_type=F32).astype(BF16)
    for hh in range(PEER_HEADS):
        for half in range(2):
            r0 = (hh * 2 + half) * PEER_N_KEYS
            st_ref[r0:r0 + PEER_N_KEYS, :] = lax.dot_general(
                sk_ref[half], q[:, r0:r0 + PEER_N_KEYS], _NT, preferred_element_type=F32)


def _post_attn(oa, ob, x2, mod3, ga, gb, wo, l1g, l1b, wpq, sk, seq):
    t, d = x2.shape
    tm = TM_PROJ
    per_b = seq // tm
    row = lambda i: (i, 0)
    mod_spec = lambda k: pl.BlockSpec((1, 1, d), lambda i: ((i // per_b) * 6 + k, 0, 0))
    n_sc = 2 * PEER_HEADS * PEER_N_KEYS
    return pl.pallas_call(
        _post_attn_kernel,
        grid=(t // tm,),
        in_specs=[pl.BlockSpec((tm, A_WIDTH), row), pl.BlockSpec((tm, B_WIDTH), row), pl.BlockSpec((tm, d), row),
                  mod_spec(2), mod_spec(4), mod_spec(3),
                  _resident(ga.shape), _resident(gb.shape), _resident(wo.shape),
                  _resident(l1g.shape), _resident(l1b.shape), _resident(wpq.shape), _resident(sk.shape)],
        out_specs=[pl.BlockSpec((tm, d), row), pl.BlockSpec((tm, d), row),
                   pl.BlockSpec((n_sc, tm), lambda i: (0, i))],
        out_shape=[jax.ShapeDtypeStruct((t, d), F32), jax.ShapeDtypeStruct((t, d), BF16),
                   jax.ShapeDtypeStruct((n_sc, t), F32)],
        compiler_params=_params("arbitrary"),
        name="post_attn",
    )(oa, ob, x2, mod3, mod3, mod3, ga, gb, wo, l1g, l1b, wpq, sk)


def _top16_rows(s, want_rank):
    rows = []
    rank = jnp.full(s.shape, float(PEER_TOPK), F32) if want_rank else None
    for r in range(PEER_TOPK):
        m = jnp.max(s, axis=0, keepdims=True)
        rows.append(m)
        hit = s == m
        if want_rank:
            rank = jnp.where(hit, float(r), rank)
        s = jnp.where(hit, -jnp.inf, s)
    return rows, rank


def _peer_topk_kernel(st_ref, e1_ref, cnt_ref, e2_ref, r2_ref):
    tl = st_ref.shape[1]
    sub = lax.broadcasted_iota(jnp.int32, (8, tl), 0)
    for hh in range(PEER_HEADS):
        r0 = hh * 2 * PEER_N_KEYS
        s1 = st_ref[r0:r0 + PEER_N_KEYS, :]
        s2 = st_ref[r0 + PEER_N_KEYS:r0 + 2 * PEER_N_KEYS, :]
        v1, _ = _top16_rows(s1, False)
        v2, rank2 = _top16_rows(s2, True)
        v2_lo = jnp.concatenate(v2[:8], axis=0)
        v2_hi = jnp.concatenate(v2[8:], axis=0)
        blocks = [v1[0] + v2_lo, v1[0] + v2_hi, v1[1] + v2_lo]
        for a in range(2, 8):
            blocks.append(jnp.where(sub < PEER_TOPK // (a + 1), v1[a] + v2_lo, -jnp.inf))
        blocks.append(jnp.concatenate(v1[8:], axis=0) + v2[0])
        cand = jnp.concatenate(blocks, axis=0)
        cmax = v1[0] + v2[0]
        z = jnp.zeros_like(cmax)
        tau = cmax
        for _ in range(PEER_TOPK):
            tau = jnp.max(cand, axis=0, keepdims=True)
            z = z + jnp.exp(tau - cmax)
            cand = jnp.where(cand == tau, -jnp.inf, cand)
        cnt = jnp.zeros(s1.shape, F32)
        for b in range(PEER_TOPK):
            cnt = cnt + jnp.where(s1 + v2[b] >= tau, 1.0, 0.0)
        rows = slice(hh * PEER_N_KEYS, (hh + 1) * PEER_N_KEYS)
        e1_ref[rows, :] = jnp.exp(s1 - v1[0]) * (1.0 / z)
        cnt_ref[rows, :] = cnt
        e2_ref[rows, :] = jnp.exp(s2 - v2[0]).astype(BF16)
        r2_ref[rows, :] = rank2.astype(BF16)


def _peer_topk(st):
    rows, t = st.shape
    tl = TL_TOPK
    n = PEER_HEADS * PEER_N_KEYS
    col = lambda i: (0, i)
    return pl.pallas_call(
        _peer_topk_kernel,
        grid=(t // tl,),
        in_specs=[pl.BlockSpec((rows, tl), col)],
        out_specs=[pl.BlockSpec((n, tl), col)] * 4,
        out_shape=[jax.ShapeDtypeStruct((n, t), F32), jax.ShapeDtypeStruct((n, t), F32),
                   jax.ShapeDtypeStruct((n, t), BF16), jax.ShapeDtypeStruct((n, t), BF16)],
        compiler_params=_params("arbitrary"),
        name="peer_topk",
    )(st)


def _gelu(x):
    return 0.5 * x * (1.0 + lax.erf(x * (2.0 ** -0.5)))


def _row_bf16(ref, row, cols, rows_out):
    r = jnp.broadcast_to(ref[pl.ds(row, 1), cols], (V7X_BF16_SUBLANES, cols.stop - cols.start)).astype(BF16)
    return pltpu.repeat(r, rows_out // V7X_BF16_SUBLANES, axis=0)


def _peer_dense_kernel(h2_ref, e1_ref, cnt_ref, e2_ref, r2_ref, u_ref, vt_ref, o_ref, *, te, ts):
    e = pl.program_id(1)
    n_i1 = te // PEER_N_KEYS
    tm = h2_ref.shape[0]

    @pl.when(e == 0)
    def _():
        o_ref[...] = jnp.zeros(o_ref.shape, F32)

    chains = [slice(s * ts, (s + 1) * ts) for s in range(tm // ts)]
    a_ts = [lax.dot_general(u_ref[...], h2_ref[cols, :], _NT, preferred_element_type=F32)
            for cols in chains]
    for cols, a_t in zip(chains, a_ts):
        parts = []
        for ii in range(n_i1):
            i1 = e * n_i1 + ii
            w = jnp.zeros((PEER_N_KEYS, ts), BF16)
            for hh in range(PEER_HEADS):
                rows = slice(hh * PEER_N_KEYS, (hh + 1) * PEER_N_KEYS)
                e1 = _row_bf16(e1_ref, hh * PEER_N_KEYS + i1, cols, PEER_N_KEYS)
                cnt = _row_bf16(cnt_ref, hh * PEER_N_KEYS + i1, cols, PEER_N_KEYS)
                w = w + jnp.where(r2_ref[rows, cols] < cnt, e2_ref[rows, cols] * e1, jnp.zeros_like(w))
            parts.append(_gelu(a_t[ii * PEER_N_KEYS:(ii + 1) * PEER_N_KEYS, :]).astype(BF16) * w)
        g_t = jnp.concatenate(parts, axis=0)
        o_ref[:, cols] += jnp.dot(vt_ref[...], g_t, preferred_element_type=F32)


def _peer_dense(h2, e1, cnt, e2, r2, u_bf, vt_bf):
    t, d = h2.shape
    n_e = u_bf.shape[0]
    n = e1.shape[0]
    tm, te = TM_PEER, TE_PEER
    tok = lambda i, e: (0, i)
    once = pl.Buffered(1)
    return pl.pallas_call(
        functools.partial(_peer_dense_kernel, te=te, ts=TS_PEER),
        grid=(t // tm, n_e // te),
        in_specs=[pl.BlockSpec((tm, d), lambda i, e: (i, 0), pipeline_mode=once),
                  pl.BlockSpec((n, tm), tok, pipeline_mode=once), pl.BlockSpec((n, tm), tok, pipeline_mode=once),
                  pl.BlockSpec((n, tm), tok, pipeline_mode=once), pl.BlockSpec((n, tm), tok, pipeline_mode=once),
                  pl.BlockSpec((te, d), lambda i, e: (e, 0)),
                  pl.BlockSpec((d, te), lambda i, e: (0, e))],
        out_specs=pl.BlockSpec((d, tm), tok, pipeline_mode=once),
        out_shape=jax.ShapeDtypeStruct((d, t), F32),
        compiler_params=_params("arbitrary", "arbitrary"),
        name="peer_dense",
    )(h2, e1, cnt, e2, r2, u_bf, vt_bf)


def _final_ln_kernel(x1_ref, ft_ref, g2_ref, lg_ref, lb_ref, o_ref):
    o_ref[...] = _layer_norm(DN_ALPHA * x1_ref[...] + g2_ref[0] * ft_ref[...].T) * lg_ref[...] + lb_ref[...]


def _final_ln(x1, ffn_t, mod3, lg, lb, seq):
    t, d = x1.shape
    tm = TM_FINAL
    per_b = seq // tm
    row = lambda i: (i, 0)
    return pl.pallas_call(
        _final_ln_kernel,
        grid=(t // tm,),
        in_specs=[pl.BlockSpec((tm, d), row), pl.BlockSpec((d, tm), lambda i: (0, i)),
                  pl.BlockSpec((1, 1, d), lambda i: ((i // per_b) * 6 + 5, 0, 0)),
                  _resident(lg.shape), _resident(lb.shape)],
        out_specs=pl.BlockSpec((tm, d), row),
        out_shape=jax.ShapeDtypeStruct((t, d), F32),
        compiler_params=_params("arbitrary"),
        name="final_ln",
    )(x1, ffn_t, mod3, lg, lb)


def _rope_tables(seq, rot_dim):
    half = rot_dim // 2
    pos = jnp.arange(seq, dtype=F32)
    inv = ROPE_THETA ** (-jnp.arange(half, dtype=F32) * 2.0 / rot_dim)
    ang = pos[:, None] * inv[None, :]
    cos, sin = jnp.cos(ang), jnp.sin(ang)
    pad = V7X_LANES - rot_dim
    ones = jnp.ones((seq, pad), F32)
    zeros = jnp.zeros((seq, pad), F32)
    zh = jnp.zeros((seq, half), F32)
    return (jnp.concatenate([cos, cos, ones], 1),
            jnp.concatenate([-sin, zh, zeros], 1),
            jnp.concatenate([zh, sin, zeros], 1))


def _layer(x2, mod3, seq, w_in, g_q_lat, g_kv_lat, w_uq, w_uk, w_uv, g_out_a, g_out_b, w_o,
           ln1_g, ln1_b, w_pq, sub_key_1, sub_key_2, u_table, v_table, ln2_g, ln2_b):
    d = x2.shape[1]
    w_bf = w_in.astype(BF16)
    w_main = jnp.concatenate([w_bf[:, :2 * A_WIDTH], w_bf[:, 3 * A_WIDTH:]], axis=1)
    w_main = jnp.pad(w_main, ((0, 0), (0, (-w_main.shape[1]) % V7X_LANES)))
    wvt = w_bf[:, 2 * A_WIDTH:3 * A_WIDTH].T
    pad_q = MLA_QK_PAD - MLA_NOPE - MLA_ROPE
    wuq_r = jnp.pad(w_uq.astype(BF16).reshape(MLA_RANK, B_HEADS, MLA_NOPE + MLA_ROPE),
                    ((0, 0), (0, 0), (0, pad_q))).reshape(MLA_RANK, B_HEADS * MLA_QK_PAD)
    tabs_a = _rope_tables(seq, A_ROT_DIM)
    tabs_b = _rope_tables(seq, MLA_ROPE)
    qa, ka, vat, qb, kb, vbt = _in_proj(
        x2, mod3, w_main, wvt, wuq_r, w_uk.astype(BF16), w_uv.astype(BF16).T,
        g_q_lat.reshape(1, -1), g_kv_lat.reshape(1, -1), tabs_a, tabs_b, seq)

    bias_mix, bias_causal = _bias_tables(TQ_ATTN, seq)
    oa = _attention(qa, ka, vat, bias_mix, seq=seq, heads=A_HEADS, ek=HEAD_DIM, bias_all_keys=True)
    ob = _attention(qb, kb, vbt, bias_causal, seq=seq, heads=B_HEADS, ek=MLA_QK_PAD, bias_all_keys=False)

    sk = jnp.stack([sub_key_1, sub_key_2]).astype(BF16)
    x1, h2, st = _post_attn(oa, ob, x2, mod3, g_out_a.reshape(1, -1), g_out_b.reshape(1, -1),
                            w_o.astype(BF16), ln1_g.reshape(1, -1), ln1_b.reshape(1, -1),
                            w_pq.astype(BF16), sk, seq)
    e1, cnt, e2, r2 = _peer_topk(st)
    ffn_t = _peer_dense(h2, e1, cnt, e2, r2, u_table.astype(BF16), v_table.astype(BF16).T)
    return _final_ln(x1, ffn_t, mod3, ln2_g.reshape(1, d), ln2_b.reshape(1, d), seq)


def kernel(x, c, w_ada, b_ada, w_in, g_q_lat, g_kv_lat, w_uq, w_uk, w_uv, g_out_a, g_out_b, w_o, ln1_g, ln1_b,
           w_pq, sub_key_1, sub_key_2, u_table, v_table, ln2_g, ln2_b):
    bsz, seq, d = x.shape
    x2 = x.reshape(bsz * seq, d)
    for l in range(w_ada.shape[0]):
        mod3 = _ada_mod(c, w_ada[l], b_ada[l]).reshape(bsz * 6, 1, d)
        x2 = _layer(x2, mod3, seq, w_in[l], g_q_lat[l], g_kv_lat[l], w_uq[l], w_uk[l], w_uv[l],
                    g_out_a[l], g_out_b[l], w_o[l], ln1_g[l], ln1_b[l], w_pq[l], sub_key_1[l], sub_key_2[l],
                    u_table[l], v_table[l], ln2_g[l], ln2_b[l])
    return x2.reshape(bsz, seq, d)
```

```python
import functools
import math

import jax
import jax.numpy as jnp
from jax import lax
from jax.experimental import pallas as pl
from jax.experimental.pallas import tpu as pltpu

F32 = jnp.float32
BF16 = jnp.bfloat16
F8 = jnp.float8_e4m3fn

D_MODEL = 2048
HEAD_DIM = 128
A_HEADS = 8
A_WIDTH = A_HEADS * HEAD_DIM
A_ROT_DIM = 32
ROPE_THETA = 500000.0
B_HEADS = 8
MLA_RANK = 512
MLA_NOPE = 128
MLA_ROPE = 64
MLA_V = 128
B_WIDTH = B_HEADS * MLA_V
MLA_QK_PAD = 256
PEER_HEADS = 8
PEER_N_KEYS = 128
PEER_TOPK = 16
DN_ALPHA = 2.0 ** 0.25

V7X_LANES = 128
V7X_BF16_SUBLANES = 16
V7X_VMEM_LIMIT = 56 * 1024 * 1024

TM_PROJ = 256
TQ_ATTN = 256
TL_TOPK = 256
TM_PEER = 1024
TS_PEER = 512
TE_PEER = 1024
PIECE_PEER = 256
TM_FINAL = 512

NEG_BIG = -1e30
LOG2E = math.log2(math.e)

_NT = (((1,), (1,)), ((), ()))

_C_QA, _C_KA, _C_CQ, _C_CKV, _C_KR = 0, A_WIDTH, 2 * A_WIDTH, 2 * A_WIDTH + MLA_RANK, 2 * A_WIDTH + 2 * MLA_RANK


def _params(*sem):
    return pltpu.CompilerParams(dimension_semantics=sem, vmem_limit_bytes=V7X_VMEM_LIMIT)


def _resident(shape):
    nd = len(shape)
    return pl.BlockSpec(shape, lambda *_: (0,) * nd, pipeline_mode=pl.Buffered(1))


def _layer_norm(x, eps=1e-5):
    mu = jnp.mean(x, axis=-1, keepdims=True)
    xc = x - mu
    var = jnp.mean(xc * xc, axis=-1, keepdims=True)
    return xc * lax.rsqrt(var + eps)


def _rms_norm(x, g, eps=1e-6):
    return x * lax.rsqrt(jnp.mean(x * x, axis=-1, keepdims=True) + eps) * g


def _rope_lanes(x, cos, sin_lo, sin_hi, half):
    return (x * cos + pltpu.roll(x, V7X_LANES - half, 1) * sin_lo + pltpu.roll(x, half, 1) * sin_hi)


def _ada_kernel(c_ref, w_ref, b_ref, o_ref):
    c = c_ref[...]
    act = (c / (1.0 + jnp.exp(-c))).astype(BF16)
    o_ref[...] = jnp.dot(act, w_ref[...].astype(BF16), preferred_element_type=F32) + b_ref[...]


def _ada_mod(c, w_ada, b_ada):
    bsz, d = c.shape
    n = w_ada.shape[1]
    tn = 1024
    return pl.pallas_call(
        _ada_kernel,
        grid=(n // tn,),
        in_specs=[pl.BlockSpec((bsz, d), lambda j: (0, 0)),
                  pl.BlockSpec((d, tn), lambda j: (0, j)),
                  pl.BlockSpec((1, tn), lambda j: (0, j))],
        out_specs=pl.BlockSpec((bsz, tn), lambda j: (0, j)),
        out_shape=jax.ShapeDtypeStruct((bsz, n), F32),
        compiler_params=_params("arbitrary"),
        name="ada_mod",
    )(c, w_ada, b_ada.reshape(1, n))


def _in_proj_kernel(x_ref, sc_ref, sh_ref, w_ref, wvt_ref, wuq_ref, wuk_ref, wuvt_ref, gq_ref, gkv_ref,
                    ca_ref, sal_ref, sah_ref, cb_ref, sbl_ref, sbh_ref,
                    qa_ref, ka_ref, vat_ref, qb_ref, kb_ref, vbt_ref, *, scale_a, scale_b):
    h = (_layer_norm(x_ref[...]) * (1.0 + sc_ref[0]) + sh_ref[0]).astype(BF16)

    def proj(c0, width):
        return jnp.dot(h, w_ref[:, c0:c0 + width], preferred_element_type=F32)

    ca, sal, sah = ca_ref[...], sal_ref[...], sah_ref[...]
    cb, sbl, sbh = cb_ref[...], sbl_ref[...], sbh_ref[...]
    chunk = 512
    for c_base, out_ref, scale in ((_C_QA, qa_ref, scale_a), (_C_KA, ka_ref, None)):
        for cc in range(A_WIDTH // chunk):
            acc = proj(c_base + cc * chunk, chunk)
            for hh in range(chunk // HEAD_DIM):
                r = _rope_lanes(acc[:, hh * HEAD_DIM:(hh + 1) * HEAD_DIM], ca, sal, sah, A_ROT_DIM // 2)
                if scale is not None:
                    r = r * scale
                c0 = cc * chunk + hh * HEAD_DIM
                out_ref[:, c0:c0 + HEAD_DIM] = r.astype(BF16)
    vat_ref[...] = lax.dot_general(wvt_ref[...], h, _NT, preferred_element_type=F32).astype(BF16)
    cq = _rms_norm(proj(_C_CQ, MLA_RANK), gq_ref[...]).astype(BF16)
    ckv = _rms_norm(proj(_C_CKV, MLA_RANK), gkv_ref[...]).astype(BF16)
    kr = _rope_lanes(proj(_C_KR, V7X_LANES), cb, sbl, sbh, MLA_ROPE // 2).astype(BF16)
    for cc in range(B_HEADS * MLA_QK_PAD // chunk):
        qq = jnp.dot(cq, wuq_ref[:, cc * chunk:(cc + 1) * chunk], preferred_element_type=F32)
        for hh in range(chunk // MLA_QK_PAD):
            c0 = cc * chunk + hh * MLA_QK_PAD
            nope = qq[:, hh * MLA_QK_PAD:hh * MLA_QK_PAD + MLA_NOPE]
            rp = qq[:, hh * MLA_QK_PAD + MLA_NOPE:(hh + 1) * MLA_QK_PAD]
            qb_ref[:, c0:c0 + MLA_NOPE] = (nope * scale_b).astype(BF16)
            qb_ref[:, c0 + MLA_NOPE:c0 + MLA_QK_PAD] = (
                _rope_lanes(rp, cb, sbl, sbh, MLA_ROPE // 2) * scale_b).astype(BF16)
    for cc in range(B_WIDTH // chunk):
        kn = jnp.dot(ckv, wuk_ref[:, cc * chunk:(cc + 1) * chunk], preferred_element_type=F32)
        for hh in range(chunk // MLA_NOPE):
            head = cc * (chunk // MLA_NOPE) + hh
            kb_ref[:, head * MLA_QK_PAD:head * MLA_QK_PAD + MLA_NOPE] = (
                kn[:, hh * MLA_NOPE:(hh + 1) * MLA_NOPE].astype(BF16))
            kb_ref[:, head * MLA_QK_PAD + MLA_NOPE:(head + 1) * MLA_QK_PAD] = kr
    vbt_ref[...] = lax.dot_general(wuvt_ref[...], ckv, _NT, preferred_element_type=F32).astype(BF16)


def _in_proj(x2, mod3, w_main, wvt, wuq_r, wuk, wuvt, gq, gkv, tabs_a, tabs_b, seq):
    t, d = x2.shape
    tm = TM_PROJ
    per_b = seq // tm
    row = lambda i: (i, 0)
    col = lambda i: (0, i)
    tab = lambda i: (i % per_b, 0)
    mod_spec = lambda k: pl.BlockSpec((1, 1, d), lambda i: ((i // per_b) * 6 + k, 0, 0))
    qk_b = B_HEADS * MLA_QK_PAD
    out_specs = [pl.BlockSpec((tm, A_WIDTH), row), pl.BlockSpec((tm, A_WIDTH), row),
                 pl.BlockSpec((A_WIDTH, tm), col),
                 pl.BlockSpec((tm, qk_b), row), pl.BlockSpec((tm, qk_b), row),
                 pl.BlockSpec((B_WIDTH, tm), col)]
    out_shape = [jax.ShapeDtypeStruct(s, BF16) for s in
                 ((t, A_WIDTH), (t, A_WIDTH), (A_WIDTH, t), (t, qk_b), (t, qk_b), (B_WIDTH, t))]
    return pl.pallas_call(
        functools.partial(_in_proj_kernel, scale_a=HEAD_DIM ** -0.5 * LOG2E,
                          scale_b=(MLA_NOPE + MLA_ROPE) ** -0.5 * LOG2E),
        grid=(t // tm,),
        in_specs=[pl.BlockSpec((tm, d), row), mod_spec(1), mod_spec(0),
                  _resident(w_main.shape), _resident(wvt.shape), _resident(wuq_r.shape), _resident(wuk.shape),
                  _resident(wuvt.shape), _resident(gq.shape), _resident(gkv.shape)]
                 + [pl.BlockSpec((tm, V7X_LANES), tab)] * 6,
        out_specs=out_specs,
        out_shape=out_shape,
        compiler_params=_params("arbitrary"),
        name="in_proj",
    )(x2, mod3, mod3, w_main, wvt, wuq_r, wuk, wuvt, gq, gkv, *tabs_a, *tabs_b)


def _attn_kernel(q_ref, k_ref, vt_ref, bias_ref, o_ref, *, tq, bias_all_keys):
    seq = q_ref.shape[0]
    nq = seq // tq
    nb = bias_ref.shape[0]

    def key_parts(qi):
        n = (qi + 1) * tq
        if bias_all_keys:
            return [(0, n, True)]
        return ([(0, n - tq, False)] if qi > 0 else []) + [(n - tq, n, True)]

    def score_tiles(qi):
        q = q_ref[qi * tq:(qi + 1) * tq, :]
        tiles = []
        for k0, k1, biased in key_parts(qi):
            st = lax.dot_general(k_ref[k0:k1, :], q, _NT, preferred_element_type=F32)
            tiles.append(st + bias_ref[nb - (k1 - k0):nb, :] if biased else st)
        return tiles

    ahead = score_tiles(0)
    for qi in range(nq):
        parts, scores = key_parts(qi), ahead
        if qi + 1 < nq:
            ahead = score_tiles(qi + 1)
        m = jnp.max(scores[0], axis=0, keepdims=True)
        for st in scores[1:]:
            m = jnp.maximum(m, jnp.max(st, axis=0, keepdims=True))
        l = jnp.zeros_like(m)
        o_t = jnp.zeros((vt_ref.shape[0], tq), F32)
        for (k0, k1, _), st in zip(parts, scores):
            p = jnp.exp2(st - m)
            l = l + jnp.sum(p, axis=0, keepdims=True)
            o_t = o_t + jnp.dot(vt_ref[:, k0:k1], p.astype(BF16), preferred_element_type=F32)
        o_ref[qi * tq:(qi + 1) * tq, :] = (o_t * (1.0 / l)).T


def _attention(q, k, vt, bias, *, seq, heads, ek, bias_all_keys):
    t = q.shape[0]
    bsz = t // seq
    ev = vt.shape[0] // heads
    return pl.pallas_call(
        functools.partial(_attn_kernel, tq=TQ_ATTN, bias_all_keys=bias_all_keys),
        grid=(bsz, heads),
        in_specs=[pl.BlockSpec((seq, ek), lambda b, h: (b, h)),
                  pl.BlockSpec((seq, ek), lambda b, h: (b, h)),
                  pl.BlockSpec((ev, seq), lambda b, h: (h, b)),
                  _resident(bias.shape)],
        out_specs=pl.BlockSpec((seq, ev), lambda b, h: (b, h)),
        out_shape=jax.ShapeDtypeStruct((t, heads * ev), F32),
        compiler_params=_params("arbitrary", "arbitrary"),
        name="attention",
    )(q, k, vt, bias)


def _bias_tables(tq, seq):
    d = ((seq - tq) + jnp.arange(tq, dtype=jnp.int32)[None, :] - jnp.arange(seq, dtype=jnp.int32)[:, None])
    ok = d >= 0
    mult = ((d <= 128).astype(F32) + ((d % 4 == 0) & (d <= 512)).astype(F32)
            + ((d % 16 == 0) & (d <= 2048)).astype(F32))
    mix = jnp.where(ok & (mult > 0), jnp.log2(jnp.maximum(mult, 1.0)), NEG_BIG).astype(F32)
    causal = jnp.where(ok[seq - tq:], 0.0, NEG_BIG).astype(F32)
    return mix, causal


def _post_attn_kernel(oa_ref, ob_ref, x_ref, g1_ref, sc_ref, sh_ref, ga_ref, gb_ref, wo_ref, l1g_ref, l1b_ref,
                      wpq_ref, sk_ref, x1_ref, h2_ref, st_ref):
    na = _rms_norm(oa_ref[...], ga_ref[...]).astype(BF16)
    nb = _rms_norm(ob_ref[...], gb_ref[...]).astype(BF16)
    mix = (jnp.dot(na, wo_ref[0:A_WIDTH, :], preferred_element_type=F32)
           + jnp.dot(nb, wo_ref[A_WIDTH:A_WIDTH + B_WIDTH, :], preferred_element_type=F32))
    x1 = _layer_norm(DN_ALPHA * x_ref[...] + g1_ref[0] * mix) * l1g_ref[...] + l1b_ref[...]
    x1_ref[...] = x1
    h2_f32 = _layer_norm(x1) * (1.0 + sc_ref[0]) + sh_ref[0]
    h2_ref[...] = h2_f32.astype(F8)
    q = jnp.dot(h2_f32.astype(BF16), wpq_ref[...], preferred_element_type=F32).astype(BF16)
    for hh in range(PEER_HEADS):
        for half in range(2):
            r0 = (hh * 2 + half) * PEER_N_KEYS
            st_ref[r0:r0 + PEER_N_KEYS, :] = lax.dot_general(
                sk_ref[half], q[:, r0:r0 + PEER_N_KEYS], _NT, preferred_element_type=F32)


def _post_attn(oa, ob, x2, mod3, ga, gb, wo, l1g, l1b, wpq, sk, seq):
    t, d = x2.shape
    tm = TM_PROJ
    per_b = seq // tm
    row = lambda i: (i, 0)
    mod_spec = lambda k: pl.BlockSpec((1, 1, d), lambda i: ((i // per_b) * 6 + k, 0, 0))
    n_sc = 2 * PEER_HEADS * PEER_N_KEYS
    return pl.pallas_call(
        _post_attn_kernel,
        grid=(t // tm,),
        in_specs=[pl.BlockSpec((tm, A_WIDTH), row), pl.BlockSpec((tm, B_WIDTH), row), pl.BlockSpec((tm, d), row),
                  mod_spec(2), mod_spec(4), mod_spec(3),
                  _resident(ga.shape), _resident(gb.shape), _resident(wo.shape),
                  _resident(l1g.shape), _resident(l1b.shape), _resident(wpq.shape), _resident(sk.shape)],
        out_specs=[pl.BlockSpec((tm, d), row), pl.BlockSpec((tm, d), row),
                   pl.BlockSpec((n_sc, tm), lambda i: (0, i))],
        out_shape=[jax.ShapeDtypeStruct((t, d), F32), jax.ShapeDtypeStruct((t, d), F8),
                   jax.ShapeDtypeStruct((n_sc, t), F32)],
        compiler_params=_params("arbitrary"),
        name="post_attn",
    )(oa, ob, x2, mod3, mod3, mod3, ga, gb, wo, l1g, l1b, wpq, sk)


def _top16_rows(s, want_rank):
    rows = []
    rank = jnp.full(s.shape, float(PEER_TOPK), F32) if want_rank else None
    for r in range(PEER_TOPK):
        m = jnp.max(s, axis=0, keepdims=True)
        rows.append(m)
        hit = s == m
        if want_rank:
            rank = jnp.where(hit, float(r), rank)
        s = jnp.where(hit, -jnp.inf, s)
    return rows, rank


def _peer_topk_kernel(st_ref, e1_ref, cnt_ref, e2_ref, r2_ref):
    tl = st_ref.shape[1]
    sub = lax.broadcasted_iota(jnp.int32, (8, tl), 0)
    for hh in range(PEER_HEADS):
        r0 = hh * 2 * PEER_N_KEYS
        s1 = st_ref[r0:r0 + PEER_N_KEYS, :]
        s2 = st_ref[r0 + PEER_N_KEYS:r0 + 2 * PEER_N_KEYS, :]
        v1, _ = _top16_rows(s1, False)
        v2, rank2 = _top16_rows(s2, True)
        v2_lo = jnp.concatenate(v2[:8], axis=0)
        v2_hi = jnp.concatenate(v2[8:], axis=0)
        blocks = [v1[0] + v2_lo, v1[0] + v2_hi, v1[1] + v2_lo]
        for a in range(2, 8):
            blocks.append(jnp.where(sub < PEER_TOPK // (a + 1), v1[a] + v2_lo, -jnp.inf))
        blocks.append(jnp.concatenate(v1[8:], axis=0) + v2[0])
        cand = jnp.concatenate(blocks, axis=0)
        cmax = v1[0] + v2[0]
        z = jnp.zeros_like(cmax)
        tau = cmax
        for _ in range(PEER_TOPK):
            tau = jnp.max(cand, axis=0, keepdims=True)
            z = z + jnp.exp(tau - cmax)
            cand = jnp.where(cand == tau, -jnp.inf, cand)
        cnt = jnp.zeros(s1.shape, F32)
        for b in range(PEER_TOPK):
            cnt = cnt + jnp.where(s1 + v2[b] >= tau, 1.0, 0.0)
        rows = slice(hh * PEER_N_KEYS, (hh + 1) * PEER_N_KEYS)
        e1_ref[rows, :] = jnp.exp(s1 - v1[0]) * (1.0 / z)
        cnt_ref[rows, :] = cnt
        e2_ref[rows, :] = jnp.exp(s2 - v2[0]).astype(BF16)
        r2_ref[rows, :] = rank2.astype(BF16)


def _peer_topk(st):
    rows, t = st.shape
    tl = TL_TOPK
    n = PEER_HEADS * PEER_N_KEYS
    col = lambda i: (0, i)
    return pl.pallas_call(
        _peer_topk_kernel,
        grid=(t // tl,),
        in_specs=[pl.BlockSpec((rows, tl), col)],
        out_specs=[pl.BlockSpec((n, tl), col)] * 4,
        out_shape=[jax.ShapeDtypeStruct((n, t), F32), jax.ShapeDtypeStruct((n, t), F32),
                   jax.ShapeDtypeStruct((n, t), BF16), jax.ShapeDtypeStruct((n, t), BF16)],
        compiler_params=_params("arbitrary"),
        name="peer_topk",
    )(st)


def _gelu(x):
    return 0.5 * x * (1.0 + lax.erf(x * (2.0 ** -0.5)))


def _row_bf16(ref, row, cols, rows_out):
    r = jnp.broadcast_to(ref[pl.ds(row, 1), cols], (V7X_BF16_SUBLANES, cols.stop - cols.start)).astype(BF16)
    return jnp.concatenate([r] * (rows_out // V7X_BF16_SUBLANES), axis=0)


def _peer_dense_kernel(h2_ref, e1_ref, cnt_ref, e2_ref, r2_ref, u_ref, vt_ref, o_ref, *, te, ts):
    e = pl.program_id(1)
    n_i1 = te // PEER_N_KEYS
    tm = h2_ref.shape[0]

    @pl.when(e == 0)
    def _():
        o_ref[...] = jnp.zeros(o_ref.shape, F32)

    chains = [slice(s * ts, (s + 1) * ts) for s in range(tm // ts)]
    d_model = vt_ref.shape[0]
    n_piece = te // PIECE_PEER
    per_piece = PIECE_PEER // PEER_N_KEYS
    d_piece = d_model // n_piece

    def gate_tile(cols, ii):
        i1 = e * n_i1 + ii
        w = jnp.zeros((PEER_N_KEYS, ts), BF16)
        for hh in range(PEER_HEADS):
            rows = slice(hh * PEER_N_KEYS, (hh + 1) * PEER_N_KEYS)
            e1 = _row_bf16(e1_ref, hh * PEER_N_KEYS + i1, cols, PEER_N_KEYS)
            cnt = _row_bf16(cnt_ref, hh * PEER_N_KEYS + i1, cols, PEER_N_KEYS)
            w = w + jnp.where(r2_ref[rows, cols] < cnt, e2_ref[rows, cols] * e1, jnp.zeros_like(w))
        return w

    def first_piece(cols, k):
        return lax.dot_general(u_ref[k * PIECE_PEER:(k + 1) * PIECE_PEER, :], h2_ref[cols, :], _NT,
                               preferred_element_type=F32)

    def act_tiles(a_piece, gates):
        return [(_gelu(a_piece[j * PEER_N_KEYS:(j + 1) * PEER_N_KEYS, :]).astype(BF16) * gates[j]).astype(F8)
                for j in range(per_piece)]

    def second_piece(cols, k, g_t):
        rows = slice(k * d_piece, (k + 1) * d_piece)
        o_ref[rows, cols] += jnp.dot(vt_ref[rows, :], g_t, preferred_element_type=F32)

    n_chain = len(chains)
    a_pieces, gates, g_ts = {}, {}, {}
    for b in range(n_chain + 2):
        g_parts = []
        for k in range(n_piece):
            if b < n_chain:
                a_pieces[b, k] = first_piece(chains[b], k)
            if 0 <= b - 1 < n_chain:
                g_parts += act_tiles(a_pieces[b - 1, k], gates[b - 1][k * per_piece:(k + 1) * per_piece])
            if b < n_chain:
                gates.setdefault(b, []).extend(
                    gate_tile(chains[b], k * per_piece + j) for j in range(per_piece))
            if 0 <= b - 2 < n_chain:
                second_piece(chains[b - 2], k, g_ts[b - 2])
        if g_parts:
            g_ts[b - 1] = jnp.concatenate(g_parts, axis=0)


def _peer_dense(h2, e1, cnt, e2, r2, u_bf, vt_bf):
    t, d = h2.shape
    n_e = u_bf.shape[0]
    n = e1.shape[0]
    tm, te = TM_PEER, TE_PEER
    tok = lambda i, e: (0, i)
    once = pl.Buffered(1)
    return pl.pallas_call(
        functools.partial(_peer_dense_kernel, te=te, ts=TS_PEER),
        grid=(t // tm, n_e // te),
        in_specs=[pl.BlockSpec((tm, d), lambda i, e: (i, 0), pipeline_mode=once),
                  pl.BlockSpec((n, tm), tok, pipeline_mode=once), pl.BlockSpec((n, tm), tok, pipeline_mode=once),
                  pl.BlockSpec((n, tm), tok, pipeline_mode=once), pl.BlockSpec((n, tm), tok, pipeline_mode=once),
                  pl.BlockSpec((te, d), lambda i, e: (e, 0)),
                  pl.BlockSpec((d, te), lambda i, e: (0, e))],
        out_specs=pl.BlockSpec((d, tm), tok, pipeline_mode=once),
        out_shape=jax.ShapeDtypeStruct((d, t), F32),
        compiler_params=_params("arbitrary", "arbitrary"),
        name="peer_dense",
    )(h2, e1, cnt, e2, r2, u_bf, vt_bf)


def _final_ln_kernel(x1_ref, ft_ref, g2_ref, lg_ref, lb_ref, o_ref):
    o_ref[...] = _layer_norm(DN_ALPHA * x1_ref[...] + g2_ref[0] * ft_ref[...].T) * lg_ref[...] + lb_ref[...]


def _final_ln(x1, ffn_t, mod3, lg, lb, seq):
    t, d = x1.shape
    tm = TM_FINAL
    per_b = seq // tm
    row = lambda i: (i, 0)
    return pl.pallas_call(
        _final_ln_kernel,
        grid=(t // tm,),
        in_specs=[pl.BlockSpec((tm, d), row), pl.BlockSpec((d, tm), lambda i: (0, i)),
                  pl.BlockSpec((1, 1, d), lambda i: ((i // per_b) * 6 + 5, 0, 0)),
                  _resident(lg.shape), _resident(lb.shape)],
        out_specs=pl.BlockSpec((tm, d), row),
        out_shape=jax.ShapeDtypeStruct((t, d), F32),
        compiler_params=_params("arbitrary"),
        name="final_ln",
    )(x1, ffn_t, mod3, lg, lb)


def _rope_tables(seq, rot_dim):
    half = rot_dim // 2
    pos = jnp.arange(seq, dtype=F32)
    inv = ROPE_THETA ** (-jnp.arange(half, dtype=F32) * 2.0 / rot_dim)
    ang = pos[:, None] * inv[None, :]
    cos, sin = jnp.cos(ang), jnp.sin(ang)
    pad = V7X_LANES - rot_dim
    ones = jnp.ones((seq, pad), F32)
    zeros = jnp.zeros((seq, pad), F32)
    zh = jnp.zeros((seq, half), F32)
    return (jnp.concatenate([cos, cos, ones], 1),
            jnp.concatenate([-sin, zh, zeros], 1),
            jnp.concatenate([zh, sin, zeros], 1))


def _layer(x2, mod3, seq, w_in, g_q_lat, g_kv_lat, w_uq, w_uk, w_uv, g_out_a, g_out_b, w_o,
           ln1_g, ln1_b, w_pq, sub_key_1, sub_key_2, u_table, v_table, ln2_g, ln2_b):
    d = x2.shape[1]
    w_bf = w_in.astype(BF16)
    w_main = jnp.concatenate([w_bf[:, :2 * A_WIDTH], w_bf[:, 3 * A_WIDTH:]], axis=1)
    w_main = jnp.pad(w_main, ((0, 0), (0, (-w_main.shape[1]) % V7X_LANES)))
    wvt = w_bf[:, 2 * A_WIDTH:3 * A_WIDTH].T
    pad_q = MLA_QK_PAD - MLA_NOPE - MLA_ROPE
    wuq_r = jnp.pad(w_uq.astype(BF16).reshape(MLA_RANK, B_HEADS, MLA_NOPE + MLA_ROPE),
                    ((0, 0), (0, 0), (0, pad_q))).reshape(MLA_RANK, B_HEADS * MLA_QK_PAD)
    tabs_a = _rope_tables(seq, A_ROT_DIM)
    tabs_b = _rope_tables(seq, MLA_ROPE)
    qa, ka, vat, qb, kb, vbt = _in_proj(
        x2, mod3, w_main, wvt, wuq_r, w_uk.astype(BF16), w_uv.astype(BF16).T,
        g_q_lat.reshape(1, -1), g_kv_lat.reshape(1, -1), tabs_a, tabs_b, seq)

    bias_mix, bias_causal = _bias_tables(TQ_ATTN, seq)
    oa = _attention(qa, ka, vat, bias_mix, seq=seq, heads=A_HEADS, ek=HEAD_DIM, bias_all_keys=True)
    ob = _attention(qb, kb, vbt, bias_causal, seq=seq, heads=B_HEADS, ek=MLA_QK_PAD, bias_all_keys=False)

    sk = jnp.stack([sub_key_1, sub_key_2]).astype(BF16)
    x1, h2, st = _post_attn(oa, ob, x2, mod3, g_out_a.reshape(1, -1), g_out_b.reshape(1, -1),
                            w_o.astype(BF16), ln1_g.reshape(1, -1), ln1_b.reshape(1, -1),
                            w_pq.astype(BF16), sk, seq)
    e1, cnt, e2, r2 = _peer_topk(st)
    ffn_t = _peer_dense(h2, e1, cnt, e2, r2, u_table.astype(F8), v_table.astype(F8).T)
    return _final_ln(x1, ffn_t, mod3, ln2_g.reshape(1, d), ln2_b.reshape(1, d), seq)


def kernel(x, c, w_ada, b_ada, w_in, g_q_lat, g_kv_lat, w_uq, w_uk, w_uv, g_out_a, g_out_b, w_o, ln1_g, ln1_b,
           w_pq, sub_key_1, sub_key_2, u_table, v_table, ln2_g, ln2_b):
    bsz, seq, d = x.shape
    x2 = x.reshape(bsz * seq, d)
    for l in range(w_ada.shape[0]):
        mod3 = _ada_mod(c, w_ada[l], b_ada[l]).reshape(bsz * 6, 1, d)
        x2 = _layer(x2, mod3, seq, w_in[l], g_q_lat[l], g_kv_lat[l], w_uq[l], w_uk[l], w_uv[l],
                    g_out_a[l], g_out_b[l], w_o[l], ln1_g[l], ln1_b[l], w_pq[l], sub_key_1[l], sub_key_2[l],
                    u_table[l], v_table[l], ln2_g[l], ln2_b[l])
    return x2.reshape(bsz, seq, d)
```

```python
import functools
import math

import jax
import jax.numpy as jnp
from jax import lax
from jax.experimental import pallas as pl
from jax.experimental.pallas import tpu as pltpu

F32 = jnp.float32
BF16 = jnp.bfloat16
F8 = jnp.float8_e4m3fn

D_MODEL = 2048
HEAD_DIM = 128
A_HEADS = 8
A_WIDTH = A_HEADS * HEAD_DIM
A_ROT_DIM = 32
ROPE_THETA = 500000.0
B_HEADS = 8
MLA_RANK = 512
MLA_NOPE = 128
MLA_ROPE = 64
MLA_V = 128
B_WIDTH = B_HEADS * MLA_V
MLA_QK_PAD = 256
PEER_HEADS = 8
PEER_N_KEYS = 128
PEER_TOPK = 16
DN_ALPHA = 2.0 ** 0.25

V7X_LANES = 128
V7X_BF16_SUBLANES = 16
V7X_VMEM_LIMIT = 56 * 1024 * 1024

TM_PROJ = 256
TQ_ATTN = 512
TL_TOPK = 256
TM_PEER = 1024
TS_PEER = 512
TE_PEER = 1024
PIECE_PEER = 256
TM_FINAL = 512

NEG_BIG = -1e30
LOG2E = math.log2(math.e)

_NT = (((1,), (1,)), ((), ()))

_C_QA, _C_KA, _C_CQ, _C_CKV, _C_KR = 0, A_WIDTH, 2 * A_WIDTH, 2 * A_WIDTH + MLA_RANK, 2 * A_WIDTH + 2 * MLA_RANK


def _params(*sem):
    return pltpu.CompilerParams(dimension_semantics=sem, vmem_limit_bytes=V7X_VMEM_LIMIT)


def _resident(shape):
    nd = len(shape)
    return pl.BlockSpec(shape, lambda *_: (0,) * nd, pipeline_mode=pl.Buffered(1))


def _layer_norm(x, eps=1e-5):
    mu = jnp.mean(x, axis=-1, keepdims=True)
    xc = x - mu
    var = jnp.mean(xc * xc, axis=-1, keepdims=True)
    return xc * lax.rsqrt(var + eps)


def _rms_norm(x, g, eps=1e-6):
    return x * lax.rsqrt(jnp.mean(x * x, axis=-1, keepdims=True) + eps) * g


def _rope_lanes(x, cos, sin_lo, sin_hi, half):
    return (x * cos + pltpu.roll(x, V7X_LANES - half, 1) * sin_lo + pltpu.roll(x, half, 1) * sin_hi)


def _ada_kernel(c_ref, w_ref, b_ref, o_ref):
    c = c_ref[...]
    act = (c / (1.0 + jnp.exp(-c))).astype(BF16)
    o_ref[...] = jnp.dot(act, w_ref[...].astype(BF16), preferred_element_type=F32) + b_ref[...]


def _ada_mod(c, w_ada, b_ada):
    bsz, d = c.shape
    n = w_ada.shape[1]
    tn = 1024
    return pl.pallas_call(
        _ada_kernel,
        grid=(n // tn,),
        in_specs=[pl.BlockSpec((bsz, d), lambda j: (0, 0)),
                  pl.BlockSpec((d, tn), lambda j: (0, j)),
                  pl.BlockSpec((1, tn), lambda j: (0, j))],
        out_specs=pl.BlockSpec((bsz, tn), lambda j: (0, j)),
        out_shape=jax.ShapeDtypeStruct((bsz, n), F32),
        compiler_params=_params("arbitrary"),
        name="ada_mod",
    )(c, w_ada, b_ada.reshape(1, n))


def _in_proj_kernel(x_ref, sc_ref, sh_ref, w_ref, wvt_ref, wuq_ref, wuk_ref, wuvt_ref, gq_ref, gkv_ref,
                    ca_ref, sal_ref, sah_ref, cb_ref, sbl_ref, sbh_ref,
                    qa_ref, ka_ref, vat_ref, qb_ref, kb_ref, vbt_ref, *, scale_a, scale_b):
    h = (_layer_norm(x_ref[...]) * (1.0 + sc_ref[0]) + sh_ref[0]).astype(BF16)

    def proj(c0, width):
        return jnp.dot(h, w_ref[:, c0:c0 + width], preferred_element_type=F32)

    ca, sal, sah = ca_ref[...], sal_ref[...], sah_ref[...]
    cb, sbl, sbh = cb_ref[...], sbl_ref[...], sbh_ref[...]
    chunk = 512
    for c_base, out_ref, scale in ((_C_QA, qa_ref, scale_a), (_C_KA, ka_ref, None)):
        for cc in range(A_WIDTH // chunk):
            acc = proj(c_base + cc * chunk, chunk)
            for hh in range(chunk // HEAD_DIM):
                r = _rope_lanes(acc[:, hh * HEAD_DIM:(hh + 1) * HEAD_DIM], ca, sal, sah, A_ROT_DIM // 2)
                if scale is not None:
                    r = r * scale
                c0 = cc * chunk + hh * HEAD_DIM
                out_ref[:, c0:c0 + HEAD_DIM] = r.astype(BF16)
    vat_ref[...] = lax.dot_general(wvt_ref[...], h, _NT, preferred_element_type=F32).astype(BF16)
    cq = _rms_norm(proj(_C_CQ, MLA_RANK), gq_ref[...]).astype(BF16)
    ckv = _rms_norm(proj(_C_CKV, MLA_RANK), gkv_ref[...]).astype(BF16)
    kr = _rope_lanes(proj(_C_KR, V7X_LANES), cb, sbl, sbh, MLA_ROPE // 2).astype(BF16)
    for cc in range(B_HEADS * MLA_QK_PAD // chunk):
        qq = jnp.dot(cq, wuq_ref[:, cc * chunk:(cc + 1) * chunk], preferred_element_type=F32)
        for hh in range(chunk // MLA_QK_PAD):
            c0 = cc * chunk + hh * MLA_QK_PAD
            nope = qq[:, hh * MLA_QK_PAD:hh * MLA_QK_PAD + MLA_NOPE]
            rp = qq[:, hh * MLA_QK_PAD + MLA_NOPE:(hh + 1) * MLA_QK_PAD]
            qb_ref[:, c0:c0 + MLA_NOPE] = (nope * scale_b).astype(BF16)
            qb_ref[:, c0 + MLA_NOPE:c0 + MLA_QK_PAD] = (
                _rope_lanes(rp, cb, sbl, sbh, MLA_ROPE // 2) * scale_b).astype(BF16)
    for cc in range(B_WIDTH // chunk):
        kn = jnp.dot(ckv, wuk_ref[:, cc * chunk:(cc + 1) * chunk], preferred_element_type=F32)
        for hh in range(chunk // MLA_NOPE):
            head = cc * (chunk // MLA_NOPE) + hh
            kb_ref[:, head * MLA_QK_PAD:head * MLA_QK_PAD + MLA_NOPE] = (
                kn[:, hh * MLA_NOPE:(hh + 1) * MLA_NOPE].astype(BF16))
            kb_ref[:, head * MLA_QK_PAD + MLA_NOPE:(head + 1) * MLA_QK_PAD] = kr
    vbt_ref[...] = lax.dot_general(wuvt_ref[...], ckv, _NT, preferred_element_type=F32).astype(BF16)


def _in_proj(x2, mod3, w_main, wvt, wuq_r, wuk, wuvt, gq, gkv, tabs_a, tabs_b, seq):
    t, d = x2.shape
    tm = TM_PROJ
    per_b = seq // tm
    row = lambda i: (i, 0)
    col = lambda i: (0, i)
    tab = lambda i: (i % per_b, 0)
    mod_spec = lambda k: pl.BlockSpec((1, 1, d), lambda i: ((i // per_b) * 6 + k, 0, 0))
    qk_b = B_HEADS * MLA_QK_PAD
    out_specs = [pl.BlockSpec((tm, A_WIDTH), row), pl.BlockSpec((tm, A_WIDTH), row),
                 pl.BlockSpec((A_WIDTH, tm), col),
                 pl.BlockSpec((tm, qk_b), row), pl.BlockSpec((tm, qk_b), row),
                 pl.BlockSpec((B_WIDTH, tm), col)]
    out_shape = [jax.ShapeDtypeStruct(s, BF16) for s in
                 ((t, A_WIDTH), (t, A_WIDTH), (A_WIDTH, t), (t, qk_b), (t, qk_b), (B_WIDTH, t))]
    return pl.pallas_call(
        functools.partial(_in_proj_kernel, scale_a=HEAD_DIM ** -0.5 * LOG2E,
                          scale_b=(MLA_NOPE + MLA_ROPE) ** -0.5 * LOG2E),
        grid=(t // tm,),
        in_specs=[pl.BlockSpec((tm, d), row), mod_spec(1), mod_spec(0),
                  _resident(w_main.shape), _resident(wvt.shape), _resident(wuq_r.shape), _resident(wuk.shape),
                  _resident(wuvt.shape), _resident(gq.shape), _resident(gkv.shape)]
                 + [pl.BlockSpec((tm, V7X_LANES), tab)] * 6,
        out_specs=out_specs,
        out_shape=out_shape,
        compiler_params=_params("arbitrary"),
        name="in_proj",
    )(x2, mod3, mod3, w_main, wvt, wuq_r, wuk, wuvt, gq, gkv, *tabs_a, *tabs_b)


def _attn_kernel(q_ref, k_ref, vt_ref, bias_ref, o_ref, *, tq, bias_all_keys):
    seq = q_ref.shape[0]
    nq = seq // tq
    nb = bias_ref.shape[0]

    def key_parts(qi):
        n = (qi + 1) * tq
        if bias_all_keys:
            return [(0, n, True)]
        return ([(0, n - tq, False)] if qi > 0 else []) + [(n - tq, n, True)]

    def score_tiles(qi):
        q = q_ref[qi * tq:(qi + 1) * tq, :]
        tiles = []
        for k0, k1, biased in key_parts(qi):
            st = lax.dot_general(k_ref[k0:k1, :], q, _NT, preferred_element_type=F32)
            tiles.append(st + bias_ref[nb - (k1 - k0):nb, :] if biased else st)
        return tiles

    ahead = score_tiles(0)
    for qi in range(nq):
        parts, scores = key_parts(qi), ahead
        if qi + 1 < nq:
            ahead = score_tiles(qi + 1)
        m = jnp.max(scores[0], axis=0, keepdims=True)
        for st in scores[1:]:
            m = jnp.maximum(m, jnp.max(st, axis=0, keepdims=True))
        l = jnp.zeros_like(m)
        o_t = jnp.zeros((vt_ref.shape[0], tq), F32)
        for (k0, k1, _), st in zip(parts, scores):
            p = jnp.exp2(st - m)
            l = l + jnp.sum(p, axis=0, keepdims=True)
            o_t = o_t + jnp.dot(vt_ref[:, k0:k1], p.astype(BF16), preferred_element_type=F32)
        o_ref[qi * tq:(qi + 1) * tq, :] = (o_t * (1.0 / l)).T


def _attention(q, k, vt, bias, *, seq, heads, ek, bias_all_keys):
    t = q.shape[0]
    bsz = t // seq
    ev = vt.shape[0] // heads
    return pl.pallas_call(
        functools.partial(_attn_kernel, tq=TQ_ATTN, bias_all_keys=bias_all_keys),
        grid=(bsz, heads),
        in_specs=[pl.BlockSpec((seq, ek), lambda b, h: (b, h)),
                  pl.BlockSpec((seq, ek), lambda b, h: (b, h)),
                  pl.BlockSpec((ev, seq), lambda b, h: (h, b)),
                  _resident(bias.shape)],
        out_specs=pl.BlockSpec((seq, ev), lambda b, h: (b, h)),
        out_shape=jax.ShapeDtypeStruct((t, heads * ev), F32),
        compiler_params=_params("arbitrary", "arbitrary"),
        name="attention",
    )(q, k, vt, bias)


def _bias_tables(tq, seq):
    d = ((seq - tq) + jnp.arange(tq, dtype=jnp.int32)[None, :] - jnp.arange(seq, dtype=jnp.int32)[:, None])
    ok = d >= 0
    mult = ((d <= 128).astype(F32) + ((d % 4 == 0) & (d <= 512)).astype(F32)
            + ((d % 16 == 0) & (d <= 2048)).astype(F32))
    mix = jnp.where(ok & (mult > 0), jnp.log2(jnp.maximum(mult, 1.0)), NEG_BIG).astype(F32)
    causal = jnp.where(ok[seq - tq:], 0.0, NEG_BIG).astype(F32)
    return mix, causal


def _post_attn_kernel(oa_ref, ob_ref, x_ref, g1_ref, sc_ref, sh_ref, ga_ref, gb_ref, wo_ref, l1g_ref, l1b_ref,
                      wpq_ref, sk_ref, x1_ref, h2_ref, st_ref):
    na = _rms_norm(oa_ref[...], ga_ref[...]).astype(BF16)
    nb = _rms_norm(ob_ref[...], gb_ref[...]).astype(BF16)
    mix = (jnp.dot(na, wo_ref[0:A_WIDTH, :], preferred_element_type=F32)
           + jnp.dot(nb, wo_ref[A_WIDTH:A_WIDTH + B_WIDTH, :], preferred_element_type=F32))
    x1 = _layer_norm(DN_ALPHA * x_ref[...] + g1_ref[0] * mix) * l1g_ref[...] + l1b_ref[...]
    x1_ref[...] = x1
    h2_f32 = _layer_norm(x1) * (1.0 + sc_ref[0]) + sh_ref[0]
    h2_ref[...] = h2_f32.astype(F8)
    q = jnp.dot(h2_f32.astype(BF16), wpq_ref[...], preferred_element_type=F32).astype(BF16)
    for hh in range(PEER_HEADS):
        for half in range(2):
            r0 = (hh * 2 + half) * PEER_N_KEYS
            st_ref[r0:r0 + PEER_N_KEYS, :] = lax.dot_general(
                sk_ref[half], q[:, r0:r0 + PEER_N_KEYS], _NT, preferred_element_type=F32)


def _post_attn(oa, ob, x2, mod3, ga, gb, wo, l1g, l1b, wpq, sk, seq):
    t, d = x2.shape
    tm = TM_PROJ
    per_b = seq // tm
    row = lambda i: (i, 0)
    mod_spec = lambda k: pl.BlockSpec((1, 1, d), lambda i: ((i // per_b) * 6 + k, 0, 0))
    n_sc = 2 * PEER_HEADS * PEER_N_KEYS
    return pl.pallas_call(
        _post_attn_kernel,
        grid=(t // tm,),
        in_specs=[pl.BlockSpec((tm, A_WIDTH), row), pl.BlockSpec((tm, B_WIDTH), row), pl.BlockSpec((tm, d), row),
                  mod_spec(2), mod_spec(4), mod_spec(3),
                  _resident(ga.shape), _resident(gb.shape), _resident(wo.shape),
                  _resident(l1g.shape), _resident(l1b.shape), _resident(wpq.shape), _resident(sk.shape)],
        out_specs=[pl.BlockSpec((tm, d), row), pl.BlockSpec((tm, d), row),
                   pl.BlockSpec((n_sc, tm), lambda i: (0, i))],
        out_shape=[jax.ShapeDtypeStruct((t, d), F32), jax.ShapeDtypeStruct((t, d), F8),
                   jax.ShapeDtypeStruct((n_sc, t), F32)],
        compiler_params=_params("arbitrary"),
        name="post_attn",
    )(oa, ob, x2, mod3, mod3, mod3, ga, gb, wo, l1g, l1b, wpq, sk)


def _top16_rows(s, want_rank):
    rows = []
    rank = jnp.full(s.shape, float(PEER_TOPK), F32) if want_rank else None
    for r in range(PEER_TOPK):
        m = jnp.max(s, axis=0, keepdims=True)
        rows.append(m)
        hit = s == m
        if want_rank:
            rank = jnp.where(hit, float(r), rank)
        s = jnp.where(hit, -jnp.inf, s)
    return rows, rank


def _peer_topk_kernel(st_ref, e1_ref, cnt_ref, e2_ref, r2_ref):
    tl = st_ref.shape[1]
    sub = lax.broadcasted_iota(jnp.int32, (8, tl), 0)
    for hh in range(PEER_HEADS):
        r0 = hh * 2 * PEER_N_KEYS
        s1 = st_ref[r0:r0 + PEER_N_KEYS, :]
        s2 = st_ref[r0 + PEER_N_KEYS:r0 + 2 * PEER_N_KEYS, :]
        v1, _ = _top16_rows(s1, False)
        v2, rank2 = _top16_rows(s2, True)
        v2_lo = jnp.concatenate(v2[:8], axis=0)
        v2_hi = jnp.concatenate(v2[8:], axis=0)
        blocks = [v1[0] + v2_lo, v1[0] + v2_hi, v1[1] + v2_lo]
        for a in range(2, 8):
            blocks.append(jnp.where(sub < PEER_TOPK // (a + 1), v1[a] + v2_lo, -jnp.inf))
        blocks.append(jnp.concatenate(v1[8:], axis=0) + v2[0])
        cand = jnp.concatenate(blocks, axis=0)
        cmax = v1[0] + v2[0]
        z = jnp.zeros_like(cmax)
        tau = cmax
        for _ in range(PEER_TOPK):
            tau = jnp.max(cand, axis=0, keepdims=True)
            z = z + jnp.exp(tau - cmax)
            cand = jnp.where(cand == tau, -jnp.inf, cand)
        cnt = jnp.zeros(s1.shape, F32)
        for b in range(PEER_TOPK):
            cnt = cnt + jnp.where(s1 + v2[b] >= tau, 1.0, 0.0)
        rows = slice(hh * PEER_N_KEYS, (hh + 1) * PEER_N_KEYS)
        e1_ref[rows, :] = jnp.exp(s1 - v1[0]) * (1.0 / z)
        cnt_ref[rows, :] = cnt
        e2_ref[rows, :] = jnp.exp(s2 - v2[0]).astype(BF16)
        r2_ref[rows, :] = rank2.astype(BF16)


def _peer_topk(st):
    rows, t = st.shape
    tl = TL_TOPK
    n = PEER_HEADS * PEER_N_KEYS
    col = lambda i: (0, i)
    return pl.pallas_call(
        _peer_topk_kernel,
        grid=(t // tl,),
        in_specs=[pl.BlockSpec((rows, tl), col)],
        out_specs=[pl.BlockSpec((n, tl), col)] * 4,
        out_shape=[jax.ShapeDtypeStruct((n, t), F32), jax.ShapeDtypeStruct((n, t), F32),
                   jax.ShapeDtypeStruct((n, t), BF16), jax.ShapeDtypeStruct((n, t), BF16)],
        compiler_params=_params("arbitrary"),
        name="peer_topk",
    )(st)


def _gelu(x):
    half = 0.5 * x
    return half + half * lax.erf(x * (2.0 ** -0.5))


def _row_bf16(ref, row, cols, rows_out):
    r = jnp.broadcast_to(ref[pl.ds(row, 1), cols], (V7X_BF16_SUBLANES, cols.stop - cols.start)).astype(BF16)
    return jnp.concatenate([r] * (rows_out // V7X_BF16_SUBLANES), axis=0)


def _peer_dense_kernel(h2_ref, e1_ref, cnt_ref, e2_ref, r2_ref, u_ref, vt_ref, o_ref, *, te, ts):
    e = pl.program_id(1)
    n_i1 = te // PEER_N_KEYS
    tm = h2_ref.shape[0]

    @pl.when(e == 0)
    def _():
        o_ref[...] = jnp.zeros(o_ref.shape, F32)

    chains = [slice(s * ts, (s + 1) * ts) for s in range(tm // ts)]
    d_model = vt_ref.shape[0]
    n_piece = te // PIECE_PEER
    per_piece = PIECE_PEER // PEER_N_KEYS
    d_piece = d_model // n_piece

    def gate_tile(cols, ii):
        i1 = e * n_i1 + ii
        w = jnp.zeros((PEER_N_KEYS, ts), BF16)
        for hh in range(PEER_HEADS):
            rows = slice(hh * PEER_N_KEYS, (hh + 1) * PEER_N_KEYS)
            e1 = _row_bf16(e1_ref, hh * PEER_N_KEYS + i1, cols, PEER_N_KEYS)
            cnt = _row_bf16(cnt_ref, hh * PEER_N_KEYS + i1, cols, PEER_N_KEYS)
            w = w + jnp.where(r2_ref[rows, cols] < cnt, e2_ref[rows, cols], jnp.zeros_like(w)) * e1
        return w

    def first_piece(cols, k):
        return lax.dot_general(u_ref[k * PIECE_PEER:(k + 1) * PIECE_PEER, :], h2_ref[cols, :], _NT,
                               preferred_element_type=F32)

    def act_tiles(a_piece, gates):
        return [(_gelu(a_piece[j * PEER_N_KEYS:(j + 1) * PEER_N_KEYS, :].astype(BF16)) * gates[j]).astype(F8)
                for j in range(per_piece)]

    def second_piece(cols, k, g_t):
        rows = slice(k * d_piece, (k + 1) * d_piece)
        o_ref[rows, cols] += jnp.dot(vt_ref[rows, :], g_t, preferred_element_type=F32)

    def gates_for(cols, k):
        return [gate_tile(cols, k * per_piece + j) for j in range(per_piece)]

    a_pieces = {}
    first_g = []
    for c, cols in enumerate(chains):
        for k in range(n_piece):
            a_pieces[c, k] = first_piece(cols, k)
            if c == 0:
                first_g.append(gates_for(cols, k))
            elif c == 1:
                first_g[k] = act_tiles(a_pieces[0, k], first_g[k])
    if len(chains) == 1:
        first_g = [act_tiles(a_pieces[0, k], first_g[k]) for k in range(n_piece)]
    g_t = jnp.concatenate([t for tiles in first_g for t in tiles], axis=0)
    for c, cols in enumerate(chains):
        nxt = []
        for k in range(n_piece):
            second_piece(cols, k, g_t)
            if c + 1 < len(chains):
                nxt.append(gates_for(chains[c + 1], k))
        if c + 1 < len(chains):
            acts = [act_tiles(a_pieces[c + 1, k], nxt[k]) for k in range(n_piece)]
            g_t = jnp.concatenate([t for tiles in acts for t in tiles], axis=0)


def _peer_dense(h2, e1, cnt, e2, r2, u_bf, vt_bf):
    t, d = h2.shape
    n_e = u_bf.shape[0]
    n = e1.shape[0]
    tm, te = TM_PEER, TE_PEER
    tok = lambda i, e: (0, i)
    once = pl.Buffered(1)
    return pl.pallas_call(
        functools.partial(_peer_dense_kernel, te=te, ts=TS_PEER),
        grid=(t // tm, n_e // te),
        in_specs=[pl.BlockSpec((tm, d), lambda i, e: (i, 0)),
                  pl.BlockSpec((n, tm), tok), pl.BlockSpec((n, tm), tok),
                  pl.BlockSpec((n, tm), tok), pl.BlockSpec((n, tm), tok),
                  pl.BlockSpec((te, d), lambda i, e: (e, 0)),
                  pl.BlockSpec((d, te), lambda i, e: (0, e))],
        out_specs=pl.BlockSpec((d, tm), tok, pipeline_mode=once),
        out_shape=jax.ShapeDtypeStruct((d, t), F32),
        compiler_params=_params("arbitrary", "arbitrary"),
        name="peer_dense",
    )(h2, e1, cnt, e2, r2, u_bf, vt_bf)


def _final_ln_kernel(x1_ref, ft_ref, g2_ref, lg_ref, lb_ref, o_ref):
    o_ref[...] = _layer_norm(DN_ALPHA * x1_ref[...] + g2_ref[0] * ft_ref[...].T) * lg_ref[...] + lb_ref[...]


def _final_ln(x1, ffn_t, mod3, lg, lb, seq):
    t, d = x1.shape
    tm = TM_FINAL
    per_b = seq // tm
    row = lambda i: (i, 0)
    return pl.pallas_call(
        _final_ln_kernel,
        grid=(t // tm,),
        in_specs=[pl.BlockSpec((tm, d), row), pl.BlockSpec((d, tm), lambda i: (0, i)),
                  pl.BlockSpec((1, 1, d), lambda i: ((i // per_b) * 6 + 5, 0, 0)),
                  _resident(lg.shape), _resident(lb.shape)],
        out_specs=pl.BlockSpec((tm, d), row),
        out_shape=jax.ShapeDtypeStruct((t, d), F32),
        compiler_params=_params("arbitrary"),
        name="final_ln",
    )(x1, ffn_t, mod3, lg, lb)


def _rope_tables(seq, rot_dim):
    half = rot_dim // 2
    pos = jnp.arange(seq, dtype=F32)
    inv = ROPE_THETA ** (-jnp.arange(half, dtype=F32) * 2.0 / rot_dim)
    ang = pos[:, None] * inv[None, :]
    cos, sin = jnp.cos(ang), jnp.sin(ang)
    pad = V7X_LANES - rot_dim
    ones = jnp.ones((seq, pad), F32)
    zeros = jnp.zeros((seq, pad), F32)
    zh = jnp.zeros((seq, half), F32)
    return (jnp.concatenate([cos, cos, ones], 1),
            jnp.concatenate([-sin, zh, zeros], 1),
            jnp.concatenate([zh, sin, zeros], 1))


def _layer(x2, mod3, seq, w_in, g_q_lat, g_kv_lat, w_uq, w_uk, w_uv, g_out_a, g_out_b, w_o,
           ln1_g, ln1_b, w_pq, sub_key_1, sub_key_2, u_table, v_table, ln2_g, ln2_b):
    d = x2.shape[1]
    w_bf = w_in.astype(BF16)
    w_main = jnp.concatenate([w_bf[:, :2 * A_WIDTH], w_bf[:, 3 * A_WIDTH:]], axis=1)
    w_main = jnp.pad(w_main, ((0, 0), (0, (-w_main.shape[1]) % V7X_LANES)))
    wvt = w_bf[:, 2 * A_WIDTH:3 * A_WIDTH].T
    pad_q = MLA_QK_PAD - MLA_NOPE - MLA_ROPE
    wuq_r = jnp.pad(w_uq.astype(BF16).reshape(MLA_RANK, B_HEADS, MLA_NOPE + MLA_ROPE),
                    ((0, 0), (0, 0), (0, pad_q))).reshape(MLA_RANK, B_HEADS * MLA_QK_PAD)
    tabs_a = _rope_tables(seq, A_ROT_DIM)
    tabs_b = _rope_tables(seq, MLA_ROPE)
    qa, ka, vat, qb, kb, vbt = _in_proj(
        x2, mod3, w_main, wvt, wuq_r, w_uk.astype(BF16), w_uv.astype(BF16).T,
        g_q_lat.reshape(1, -1), g_kv_lat.reshape(1, -1), tabs_a, tabs_b, seq)

    bias_mix, bias_causal = _bias_tables(TQ_ATTN, seq)
    oa = _attention(qa, ka, vat, bias_mix, seq=seq, heads=A_HEADS, ek=HEAD_DIM, bias_all_keys=True)
    ob = _attention(qb, kb, vbt, bias_causal, seq=seq, heads=B_HEADS, ek=MLA_QK_PAD, bias_all_keys=False)

    sk = jnp.stack([sub_key_1, sub_key_2]).astype(BF16)
    x1, h2, st = _post_attn(oa, ob, x2, mod3, g_out_a.reshape(1, -1), g_out_b.reshape(1, -1),
                            w_o.astype(BF16), ln1_g.reshape(1, -1), ln1_b.reshape(1, -1),
                            w_pq.astype(BF16), sk, seq)
    e1, cnt, e2, r2 = _peer_topk(st)
    ffn_t = _peer_dense(h2, e1, cnt, e2, r2, u_table.astype(F8), v_table.astype(F8).T)
    return _final_ln(x1, ffn_t, mod3, ln2_g.reshape(1, d), ln2_b.reshape(1, d), seq)


def kernel(x, c, w_ada, b_ada, w_in, g_q_lat, g_kv_lat, w_uq, w_uk, w_uv, g_out_a, g_out_b, w_o, ln1_g, ln1_b,
           w_pq, sub_key_1, sub_key_2, u_table, v_table, ln2_g, ln2_b):
    bsz, seq, d = x.shape
    x2 = x.reshape(bsz * seq, d)
    for l in range(w_ada.shape[0]):
        mod3 = _ada_mod(c, w_ada[l], b_ada[l]).reshape(bsz * 6, 1, d)
        x2 = _layer(x2, mod3, seq, w_in[l], g_q_lat[l], g_kv_lat[l], w_uq[l], w_uk[l], w_uv[l],
                    g_out_a[l], g_out_b[l], w_o[l], ln1_g[l], ln1_b[l], w_pq[l], sub_key_1[l], sub_key_2[l],
                    u_table[l], v_table[l], ln2_g[l], ln2_b[l])
    return x2.reshape(bsz, seq, d)
```

```python
import functools
import math

import jax
import jax.numpy as jnp
from jax import lax
from jax.experimental import pallas as pl
from jax.experimental.pallas import tpu as pltpu

F32 = jnp.float32
BF16 = jnp.bfloat16
F8 = jnp.float8_e4m3fn

D_MODEL = 2048
HEAD_DIM = 128
A_HEADS = 8
A_WIDTH = A_HEADS * HEAD_DIM
A_ROT_DIM = 32
ROPE_THETA = 500000.0
B_HEADS = 8
MLA_RANK = 512
MLA_NOPE = 128
MLA_ROPE = 64
MLA_V = 128
B_WIDTH = B_HEADS * MLA_V
MLA_QK_PAD = 256
PEER_HEADS = 8
PEER_N_KEYS = 128
PEER_TOPK = 16
DN_ALPHA = 2.0 ** 0.25

V7X_LANES = 128
V7X_BF16_SUBLANES = 16
V7X_VMEM_LIMIT = 56 * 1024 * 1024

TM_PROJ = 256
TQ_ATTN = 512
TL_TOPK = 256
TM_PEER = 1024
TS_PEER = 512
TE_PEER = 1024
PIECE_PEER = 256
TM_FINAL = 512

NEG_BIG = -1e30
LOG2E = math.log2(math.e)

_NT = (((1,), (1,)), ((), ()))

_C_QA, _C_KA, _C_CQ, _C_CKV, _C_KR = 0, A_WIDTH, 2 * A_WIDTH, 2 * A_WIDTH + MLA_RANK, 2 * A_WIDTH + 2 * MLA_RANK


def _params(*sem):
    return pltpu.CompilerParams(dimension_semantics=sem, vmem_limit_bytes=V7X_VMEM_LIMIT)


def _resident(shape):
    nd = len(shape)
    return pl.BlockSpec(shape, lambda *_: (0,) * nd, pipeline_mode=pl.Buffered(1))


def _layer_norm(x, eps=1e-5):
    mu = jnp.mean(x, axis=-1, keepdims=True)
    xc = x - mu
    var = jnp.mean(xc * xc, axis=-1, keepdims=True)
    return xc * lax.rsqrt(var + eps)


def _rms_norm(x, g, eps=1e-6):
    return x * lax.rsqrt(jnp.mean(x * x, axis=-1, keepdims=True) + eps) * g


def _rope_lanes(x, cos, sin_lo, sin_hi, half):
    return (x * cos + pltpu.roll(x, V7X_LANES - half, 1) * sin_lo + pltpu.roll(x, half, 1) * sin_hi)


def _ada_kernel(c_ref, w_ref, b_ref, o_ref):
    c = c_ref[...]
    act = (c / (1.0 + jnp.exp(-c))).astype(BF16)
    o_ref[...] = jnp.dot(act, w_ref[...].astype(BF16), preferred_element_type=F32) + b_ref[...]


def _ada_mod(c, w_ada, b_ada):
    bsz, d = c.shape
    n = w_ada.shape[1]
    tn = 1024
    return pl.pallas_call(
        _ada_kernel,
        grid=(n // tn,),
        in_specs=[pl.BlockSpec((bsz, d), lambda j: (0, 0)),
                  pl.BlockSpec((d, tn), lambda j: (0, j)),
                  pl.BlockSpec((1, tn), lambda j: (0, j))],
        out_specs=pl.BlockSpec((bsz, tn), lambda j: (0, j)),
        out_shape=jax.ShapeDtypeStruct((bsz, n), F32),
        compiler_params=_params("arbitrary"),
        name="ada_mod",
    )(c, w_ada, b_ada.reshape(1, n))


def _in_proj_kernel(x_ref, sc_ref, sh_ref, w_ref, wvt_ref, wuq_ref, wuk_ref, wuvt_ref, gq_ref, gkv_ref,
                    ca_ref, sal_ref, sah_ref, cb_ref, sbl_ref, sbh_ref,
                    qa_ref, ka_ref, vat_ref, qb_ref, kb_ref, vbt_ref, *, scale_a, scale_b):
    h = (_layer_norm(x_ref[...]) * (1.0 + sc_ref[0]) + sh_ref[0]).astype(BF16)

    def proj(c0, width):
        return jnp.dot(h, w_ref[:, c0:c0 + width], preferred_element_type=F32)

    ca, sal, sah = ca_ref[...], sal_ref[...], sah_ref[...]
    cb, sbl, sbh = cb_ref[...], sbl_ref[...], sbh_ref[...]
    chunk = 512
    for c_base, out_ref, scale in ((_C_QA, qa_ref, scale_a), (_C_KA, ka_ref, None)):
        for cc in range(A_WIDTH // chunk):
            acc = proj(c_base + cc * chunk, chunk)
            for hh in range(chunk // HEAD_DIM):
                r = _rope_lanes(acc[:, hh * HEAD_DIM:(hh + 1) * HEAD_DIM], ca, sal, sah, A_ROT_DIM // 2)
                if scale is not None:
                    r = r * scale
                c0 = cc * chunk + hh * HEAD_DIM
                out_ref[:, c0:c0 + HEAD_DIM] = r.astype(BF16)
    vat_ref[...] = lax.dot_general(wvt_ref[...], h, _NT, preferred_element_type=F32).astype(BF16)
    cq = _rms_norm(proj(_C_CQ, MLA_RANK), gq_ref[...]).astype(BF16)
    ckv = _rms_norm(proj(_C_CKV, MLA_RANK), gkv_ref[...]).astype(BF16)
    kr = _rope_lanes(proj(_C_KR, V7X_LANES), cb, sbl, sbh, MLA_ROPE // 2).astype(BF16)
    for cc in range(B_HEADS * MLA_QK_PAD // chunk):
        qq = jnp.dot(cq, wuq_ref[:, cc * chunk:(cc + 1) * chunk], preferred_element_type=F32)
        for hh in range(chunk // MLA_QK_PAD):
            c0 = cc * chunk + hh * MLA_QK_PAD
            nope = qq[:, hh * MLA_QK_PAD:hh * MLA_QK_PAD + MLA_NOPE]
            rp = qq[:, hh * MLA_QK_PAD + MLA_NOPE:(hh + 1) * MLA_QK_PAD]
            qb_ref[:, c0:c0 + MLA_NOPE] = (nope * scale_b).astype(BF16)
            qb_ref[:, c0 + MLA_NOPE:c0 + MLA_QK_PAD] = (
                _rope_lanes(rp, cb, sbl, sbh, MLA_ROPE // 2) * scale_b).astype(BF16)
    for cc in range(B_WIDTH // chunk):
        kn = jnp.dot(ckv, wuk_ref[:, cc * chunk:(cc + 1) * chunk], preferred_element_type=F32)
        for hh in range(chunk // MLA_NOPE):
            head = cc * (chunk // MLA_NOPE) + hh
            kb_ref[:, head * MLA_QK_PAD:head * MLA_QK_PAD + MLA_NOPE] = (
                kn[:, hh * MLA_NOPE:(hh + 1) * MLA_NOPE].astype(BF16))
            kb_ref[:, head * MLA_QK_PAD + MLA_NOPE:(head + 1) * MLA_QK_PAD] = kr
    vbt_ref[...] = lax.dot_general(wuvt_ref[...], ckv, _NT, preferred_element_type=F32).astype(BF16)


def _in_proj(x2, mod3, w_main, wvt, wuq_r, wuk, wuvt, gq, gkv, tabs_a, tabs_b, seq):
    t, d = x2.shape
    tm = TM_PROJ
    per_b = seq // tm
    row = lambda i: (i, 0)
    col = lambda i: (0, i)
    tab = lambda i: (i % per_b, 0)
    mod_spec = lambda k: pl.BlockSpec((1, 1, d), lambda i: ((i // per_b) * 6 + k, 0, 0))
    qk_b = B_HEADS * MLA_QK_PAD
    out_specs = [pl.BlockSpec((tm, A_WIDTH), row), pl.BlockSpec((tm, A_WIDTH), row),
                 pl.BlockSpec((A_WIDTH, tm), col),
                 pl.BlockSpec((tm, qk_b), row), pl.BlockSpec((tm, qk_b), row),
                 pl.BlockSpec((B_WIDTH, tm), col)]
    out_shape = [jax.ShapeDtypeStruct(s, BF16) for s in
                 ((t, A_WIDTH), (t, A_WIDTH), (A_WIDTH, t), (t, qk_b), (t, qk_b), (B_WIDTH, t))]
    return pl.pallas_call(
        functools.partial(_in_proj_kernel, scale_a=HEAD_DIM ** -0.5 * LOG2E,
                          scale_b=(MLA_NOPE + MLA_ROPE) ** -0.5 * LOG2E),
        grid=(t // tm,),
        in_specs=[pl.BlockSpec((tm, d), row), mod_spec(1), mod_spec(0),
                  _resident(w_main.shape), _resident(wvt.shape), _resident(wuq_r.shape), _resident(wuk.shape),
                  _resident(wuvt.shape), _resident(gq.shape), _resident(gkv.shape)]
                 + [pl.BlockSpec((tm, V7X_LANES), tab)] * 6,
        out_specs=out_specs,
        out_shape=out_shape,
        compiler_params=_params("arbitrary"),
        name="in_proj",
    )(x2, mod3, mod3, w_main, wvt, wuq_r, wuk, wuvt, gq, gkv, *tabs_a, *tabs_b)


def _attn_kernel(q_ref, k_ref, vt_ref, bias_ref, o_ref, *, tq, bias_all_keys):
    seq = q_ref.shape[0]
    nq = seq // tq
    nb = bias_ref.shape[0]

    def key_parts(qi):
        n = (qi + 1) * tq
        if bias_all_keys:
            return [(0, n, True)]
        return ([(0, n - tq, False)] if qi > 0 else []) + [(n - tq, n, True)]

    def score_tiles(qi):
        q = q_ref[qi * tq:(qi + 1) * tq, :]
        tiles = []
        for k0, k1, biased in key_parts(qi):
            st = lax.dot_general(k_ref[k0:k1, :], q, _NT, preferred_element_type=F32)
            tiles.append(st + bias_ref[nb - (k1 - k0):nb, :] if biased else st)
        return tiles

    ahead = score_tiles(0)
    for qi in range(nq):
        parts, scores = key_parts(qi), ahead
        if qi + 1 < nq:
            ahead = score_tiles(qi + 1)
        m = jnp.max(scores[0], axis=0, keepdims=True)
        for st in scores[1:]:
            m = jnp.maximum(m, jnp.max(st, axis=0, keepdims=True))
        l = jnp.zeros_like(m)
        o_t = jnp.zeros((vt_ref.shape[0], tq), F32)
        for (k0, k1, _), st in zip(parts, scores):
            p = jnp.exp2(st - m)
            l = l + jnp.sum(p, axis=0, keepdims=True)
            o_t = o_t + jnp.dot(vt_ref[:, k0:k1], p.astype(BF16), preferred_element_type=F32)
        o_ref[qi * tq:(qi + 1) * tq, :] = (o_t * (1.0 / l)).T


def _attention(q, k, vt, bias, *, seq, heads, ek, bias_all_keys):
    t = q.shape[0]
    bsz = t // seq
    ev = vt.shape[0] // heads
    return pl.pallas_call(
        functools.partial(_attn_kernel, tq=TQ_ATTN, bias_all_keys=bias_all_keys),
        grid=(bsz, heads),
        in_specs=[pl.BlockSpec((seq, ek), lambda b, h: (b, h)),
                  pl.BlockSpec((seq, ek), lambda b, h: (b, h)),
                  pl.BlockSpec((ev, seq), lambda b, h: (h, b)),
                  _resident(bias.shape)],
        out_specs=pl.BlockSpec((seq, ev), lambda b, h: (b, h)),
        out_shape=jax.ShapeDtypeStruct((t, heads * ev), F32),
        compiler_params=_params("arbitrary", "arbitrary"),
        name="attention",
    )(q, k, vt, bias)


def _bias_tables(tq, seq):
    d = ((seq - tq) + jnp.arange(tq, dtype=jnp.int32)[None, :] - jnp.arange(seq, dtype=jnp.int32)[:, None])
    ok = d >= 0
    mult = ((d <= 128).astype(F32) + ((d % 4 == 0) & (d <= 512)).astype(F32)
            + ((d % 16 == 0) & (d <= 2048)).astype(F32))
    mix = jnp.where(ok & (mult > 0), jnp.log2(jnp.maximum(mult, 1.0)), NEG_BIG).astype(F32)
    causal = jnp.where(ok[seq - tq:], 0.0, NEG_BIG).astype(F32)
    return mix, causal


def _post_attn_kernel(oa_ref, ob_ref, x_ref, g1_ref, sc_ref, sh_ref, ga_ref, gb_ref, wo_ref, l1g_ref, l1b_ref,
                      wpq_ref, sk_ref, x1_ref, h2_ref, st_ref):
    na = _rms_norm(oa_ref[...], ga_ref[...]).astype(BF16)
    nb = _rms_norm(ob_ref[...], gb_ref[...]).astype(BF16)
    mix = (jnp.dot(na, wo_ref[0:A_WIDTH, :], preferred_element_type=F32)
           + jnp.dot(nb, wo_ref[A_WIDTH:A_WIDTH + B_WIDTH, :], preferred_element_type=F32))
    x1 = _layer_norm(DN_ALPHA * x_ref[...] + g1_ref[0] * mix) * l1g_ref[...] + l1b_ref[...]
    x1_ref[...] = x1
    h2_f32 = _layer_norm(x1) * (1.0 + sc_ref[0]) + sh_ref[0]
    h2_ref[...] = h2_f32.astype(F8)
    q = jnp.dot(h2_f32.astype(BF16), wpq_ref[...], preferred_element_type=F32).astype(BF16)
    for hh in range(PEER_HEADS):
        for half in range(2):
            r0 = (hh * 2 + half) * PEER_N_KEYS
            st_ref[r0:r0 + PEER_N_KEYS, :] = lax.dot_general(
                sk_ref[half], q[:, r0:r0 + PEER_N_KEYS], _NT, preferred_element_type=F32)


def _post_attn(oa, ob, x2, mod3, ga, gb, wo, l1g, l1b, wpq, sk, seq):
    t, d = x2.shape
    tm = TM_PROJ
    per_b = seq // tm
    row = lambda i: (i, 0)
    mod_spec = lambda k: pl.BlockSpec((1, 1, d), lambda i: ((i // per_b) * 6 + k, 0, 0))
    n_sc = 2 * PEER_HEADS * PEER_N_KEYS
    return pl.pallas_call(
        _post_attn_kernel,
        grid=(t // tm,),
        in_specs=[pl.BlockSpec((tm, A_WIDTH), row), pl.BlockSpec((tm, B_WIDTH), row), pl.BlockSpec((tm, d), row),
                  mod_spec(2), mod_spec(4), mod_spec(3),
                  _resident(ga.shape), _resident(gb.shape), _resident(wo.shape),
                  _resident(l1g.shape), _resident(l1b.shape), _resident(wpq.shape), _resident(sk.shape)],
        out_specs=[pl.BlockSpec((tm, d), row), pl.BlockSpec((tm, d), row),
                   pl.BlockSpec((n_sc, tm), lambda i: (0, i))],
        out_shape=[jax.ShapeDtypeStruct((t, d), F32), jax.ShapeDtypeStruct((t, d), F8),
                   jax.ShapeDtypeStruct((n_sc, t), F32)],
        compiler_params=_params("arbitrary"),
        name="post_attn",
    )(oa, ob, x2, mod3, mod3, mod3, ga, gb, wo, l1g, l1b, wpq, sk)


V7X_SUBLANES = 8
_GROUPS = PEER_N_KEYS // V7X_SUBLANES


def _oddeven_merge_sort_pairs(n):
    pairs = []
    p = 1
    while p < n:
        k = p
        while k >= 1:
            for j in range(k % p, n - k, 2 * k):
                for i in range(min(k, n - j - k)):
                    if (i + j) // (2 * p) == (i + j + k) // (2 * p):
                        pairs.append((i + j, i + j + k))
            k //= 2
        p *= 2
    return pairs


_SORT16 = _oddeven_merge_sort_pairs(_GROUPS)


def _sublane_max(x):
    for shift in (4, 2, 1):
        x = jnp.maximum(x, pltpu.roll(x, shift, 0))
    return x


def _sorted_top16(groups):
    c = list(groups)
    for i, j in _SORT16:
        c[i], c[j] = jnp.maximum(c[i], c[j]), jnp.minimum(c[i], c[j])
    out = []
    for r in range(PEER_TOPK):
        m = _sublane_max(c[0])
        out.append(m)
        if r + 1 < PEER_TOPK:
            hit = c[0] == m
            c = [jnp.where(hit, c[k + 1], c[k]) for k in range(len(c) - 1)]
    return out


def _count_leading(pred, v):
    p8 = pred(v[7])
    p4 = pred(jnp.where(p8, v[11], v[3]))
    p2 = pred(jnp.where(p8, jnp.where(p4, v[13], v[9]), jnp.where(p4, v[5], v[1])))
    hi = jnp.where(p4, jnp.where(p2, v[14], v[12]), jnp.where(p2, v[10], v[8]))
    lo = jnp.where(p4, jnp.where(p2, v[6], v[4]), jnp.where(p2, v[2], v[0]))
    p1 = pred(jnp.where(p8, hi, lo))
    p16 = pred(v[15])
    one, zero = jnp.float32(1.0), jnp.float32(0.0)
    return (jnp.where(p8, 8.0, zero) + jnp.where(p4, 4.0, zero) + jnp.where(p2, 2.0, zero)
            + jnp.where(p1, one, zero) + jnp.where(p16, one, zero))


def _peer_topk_kernel(st_ref, e1_ref, cnt_ref, e2_ref, r2_ref):
    tl = st_ref.shape[1]
    sub = lax.broadcasted_iota(jnp.int32, (V7X_SUBLANES, tl), 0)

    def by_sublane(rows):
        out = rows[0]
        for j in range(1, V7X_SUBLANES):
            out = jnp.where(sub == j, rows[j], out)
        return out

    for hh in range(PEER_HEADS):
        r0 = hh * 2 * PEER_N_KEYS
        g1 = [st_ref[r0 + V7X_SUBLANES * k:r0 + V7X_SUBLANES * (k + 1), :] for k in range(_GROUPS)]
        g2 = [st_ref[r0 + PEER_N_KEYS + V7X_SUBLANES * k:r0 + PEER_N_KEYS + V7X_SUBLANES * (k + 1), :]
              for k in range(_GROUPS)]
        v1 = _sorted_top16(g1)
        v2 = _sorted_top16(g2)
        v2_lo, v2_hi, v1_hi = by_sublane(v2[:8]), by_sublane(v2[8:]), by_sublane(v1[8:])
        cand = [v1[0] + v2_lo, v1[0] + v2_hi, v1[1] + v2_lo]
        for a in range(2, 8):
            cand.append(jnp.where(sub < PEER_TOPK // (a + 1), v1[a] + v2_lo, -jnp.inf))
        cand.append(v1_hi + v2[0])
        cmax = v1[0] + v2[0]
        z = jnp.zeros_like(cmax)
        tau = cmax
        for r in range(PEER_TOPK):
            tau = cand[0]
            for blk in cand[1:]:
                tau = jnp.maximum(tau, blk)
            tau = _sublane_max(tau)
            z = z + jnp.exp(tau - cmax)
            if r + 1 < PEER_TOPK:
                cand = [jnp.where(blk == tau, -jnp.inf, blk) for blk in cand]
        inv_z = 1.0 / z
        for k in range(_GROUPS):
            rows = slice(hh * PEER_N_KEYS + V7X_SUBLANES * k, hh * PEER_N_KEYS + V7X_SUBLANES * (k + 1))
            x1 = g1[k]
            e1_ref[rows, :] = jnp.exp(x1 - v1[0]) * inv_z
            cnt_ref[rows, :] = _count_leading(lambda vb: x1 + vb >= tau, v2)
        for k in range(0, _GROUPS, 2):
            rows = slice(hh * PEER_N_KEYS + V7X_SUBLANES * k, hh * PEER_N_KEYS + V7X_SUBLANES * (k + 2))
            e2_ref[rows, :] = jnp.concatenate(
                [jnp.exp(g2[k + j] - v2[0]) for j in range(2)], axis=0).astype(BF16)
            r2_ref[rows, :] = jnp.concatenate(
                [_count_leading(lambda vb, x=g2[k + j]: vb > x, v2) for j in range(2)], axis=0).astype(BF16)


def _peer_topk(st):
    rows, t = st.shape
    tl = TL_TOPK
    n = PEER_HEADS * PEER_N_KEYS
    col = lambda i: (0, i)
    return pl.pallas_call(
        _peer_topk_kernel,
        grid=(t // tl,),
        in_specs=[pl.BlockSpec((rows, tl), col)],
        out_specs=[pl.BlockSpec((n, tl), col)] * 4,
        out_shape=[jax.ShapeDtypeStruct((n, t), F32), jax.ShapeDtypeStruct((n, t), F32),
                   jax.ShapeDtypeStruct((n, t), BF16), jax.ShapeDtypeStruct((n, t), BF16)],
        compiler_params=_params("arbitrary"),
        name="peer_topk",
    )(st)


def _gelu(x):
    half = 0.5 * x
    return half + half * lax.erf(x * (2.0 ** -0.5))


def _row_bf16(ref, row, cols, rows_out):
    r = jnp.broadcast_to(ref[pl.ds(row, 1), cols], (V7X_BF16_SUBLANES, cols.stop - cols.start)).astype(BF16)
    return jnp.concatenate([r] * (rows_out // V7X_BF16_SUBLANES), axis=0)


def _peer_dense_kernel(h2_ref, e1_ref, cnt_ref, e2_ref, r2_ref, u_ref, vt_ref, o_ref, *, te, ts):
    e = pl.program_id(1)
    n_i1 = te // PEER_N_KEYS
    tm = h2_ref.shape[0]

    @pl.when(e == 0)
    def _():
        o_ref[...] = jnp.zeros(o_ref.shape, F32)

    chains = [slice(s * ts, (s + 1) * ts) for s in range(tm // ts)]
    d_model = vt_ref.shape[0]
    n_piece = te // PIECE_PEER
    per_piece = PIECE_PEER // PEER_N_KEYS
    d_piece = d_model // n_piece

    def gate_tile(cols, ii):
        i1 = e * n_i1 + ii
        w = jnp.zeros((PEER_N_KEYS, ts), BF16)
        for hh in range(PEER_HEADS):
            rows = slice(hh * PEER_N_KEYS, (hh + 1) * PEER_N_KEYS)
            e1 = _row_bf16(e1_ref, hh * PEER_N_KEYS + i1, cols, PEER_N_KEYS)
            cnt = _row_bf16(cnt_ref, hh * PEER_N_KEYS + i1, cols, PEER_N_KEYS)
            w = w + jnp.where(r2_ref[rows, cols] < cnt, e2_ref[rows, cols], jnp.zeros_like(w)) * e1
        return w

    def first_piece(cols, k):
        return lax.dot_general(u_ref[k * PIECE_PEER:(k + 1) * PIECE_PEER, :], h2_ref[cols, :], _NT,
                               preferred_element_type=F32)

    def act_tiles(a_piece, gates):
        return [(_gelu(a_piece[j * PEER_N_KEYS:(j + 1) * PEER_N_KEYS, :].astype(BF16)) * gates[j]).astype(F8)
                for j in range(per_piece)]

    def second_piece(cols, k, g_t):
        rows = slice(k * d_piece, (k + 1) * d_piece)
        o_ref[rows, cols] += jnp.dot(vt_ref[rows, :], g_t, preferred_element_type=F32)

    def gates_for(cols, k):
        return [gate_tile(cols, k * per_piece + j) for j in range(per_piece)]

    a_pieces = {}
    first_g = []
    for c, cols in enumerate(chains):
        for k in range(n_piece):
            a_pieces[c, k] = first_piece(cols, k)
            if c == 0:
                first_g.append(gates_for(cols, k))
            elif c == 1:
                first_g[k] = act_tiles(a_pieces[0, k], first_g[k])
    if len(chains) == 1:
        first_g = [act_tiles(a_pieces[0, k], first_g[k]) for k in range(n_piece)]
    g_t = jnp.concatenate([t for tiles in first_g for t in tiles], axis=0)
    for c, cols in enumerate(chains):
        nxt = []
        for k in range(n_piece):
            second_piece(cols, k, g_t)
            if c + 1 < len(chains):
                nxt.append(gates_for(chains[c + 1], k))
        if c + 1 < len(chains):
            acts = [act_tiles(a_pieces[c + 1, k], nxt[k]) for k in range(n_piece)]
            g_t = jnp.concatenate([t for tiles in acts for t in tiles], axis=0)


def _peer_dense(h2, e1, cnt, e2, r2, u_bf, vt_bf):
    t, d = h2.shape
    n_e = u_bf.shape[0]
    n = e1.shape[0]
    tm, te = TM_PEER, TE_PEER
    tok = lambda i, e: (0, i)
    once = pl.Buffered(1)
    return pl.pallas_call(
        functools.partial(_peer_dense_kernel, te=te, ts=TS_PEER),
        grid=(t // tm, n_e // te),
        in_specs=[pl.BlockSpec((tm, d), lambda i, e: (i, 0)),
                  pl.BlockSpec((n, tm), tok), pl.BlockSpec((n, tm), tok),
                  pl.BlockSpec((n, tm), tok), pl.BlockSpec((n, tm), tok),
                  pl.BlockSpec((te, d), lambda i, e: (e, 0)),
                  pl.BlockSpec((d, te), lambda i, e: (0, e))],
        out_specs=pl.BlockSpec((d, tm), tok, pipeline_mode=once),
        out_shape=jax.ShapeDtypeStruct((d, t), F32),
        compiler_params=_params("arbitrary", "arbitrary"),
        name="peer_dense",
    )(h2, e1, cnt, e2, r2, u_bf, vt_bf)


def _final_ln_kernel(x1_ref, ft_ref, g2_ref, lg_ref, lb_ref, o_ref):
    o_ref[...] = _layer_norm(DN_ALPHA * x1_ref[...] + g2_ref[0] * ft_ref[...].T) * lg_ref[...] + lb_ref[...]


def _final_ln(x1, ffn_t, mod3, lg, lb, seq):
    t, d = x1.shape
    tm = TM_FINAL
    per_b = seq // tm
    row = lambda i: (i, 0)
    return pl.pallas_call(
        _final_ln_kernel,
        grid=(t // tm,),
        in_specs=[pl.BlockSpec((tm, d), row), pl.BlockSpec((d, tm), lambda i: (0, i)),
                  pl.BlockSpec((1, 1, d), lambda i: ((i // per_b) * 6 + 5, 0, 0)),
                  _resident(lg.shape), _resident(lb.shape)],
        out_specs=pl.BlockSpec((tm, d), row),
        out_shape=jax.ShapeDtypeStruct((t, d), F32),
        compiler_params=_params("arbitrary"),
        name="final_ln",
    )(x1, ffn_t, mod3, lg, lb)


def _rope_tables(seq, rot_dim):
    half = rot_dim // 2
    pos = jnp.arange(seq, dtype=F32)
    inv = ROPE_THETA ** (-jnp.arange(half, dtype=F32) * 2.0 / rot_dim)
    ang = pos[:, None] * inv[None, :]
    cos, sin = jnp.cos(ang), jnp.sin(ang)
    pad = V7X_LANES - rot_dim
    ones = jnp.ones((seq, pad), F32)
    zeros = jnp.zeros((seq, pad), F32)
    zh = jnp.zeros((seq, half), F32)
    return (jnp.concatenate([cos, cos, ones], 1),
            jnp.concatenate([-sin, zh, zeros], 1),
            jnp.concatenate([zh, sin, zeros], 1))


def _layer(x2, mod3, seq, w_in, g_q_lat, g_kv_lat, w_uq, w_uk, w_uv, g_out_a, g_out_b, w_o,
           ln1_g, ln1_b, w_pq, sub_key_1, sub_key_2, u_table, v_table, ln2_g, ln2_b):
    d = x2.shape[1]
    w_bf = w_in.astype(BF16)
    w_main = jnp.concatenate([w_bf[:, :2 * A_WIDTH], w_bf[:, 3 * A_WIDTH:]], axis=1)
    w_main = jnp.pad(w_main, ((0, 0), (0, (-w_main.shape[1]) % V7X_LANES)))
    wvt = w_bf[:, 2 * A_WIDTH:3 * A_WIDTH].T
    pad_q = MLA_QK_PAD - MLA_NOPE - MLA_ROPE
    wuq_r = jnp.pad(w_uq.astype(BF16).reshape(MLA_RANK, B_HEADS, MLA_NOPE + MLA_ROPE),
                    ((0, 0), (0, 0), (0, pad_q))).reshape(MLA_RANK, B_HEADS * MLA_QK_PAD)
    tabs_a = _rope_tables(seq, A_ROT_DIM)
    tabs_b = _rope_tables(seq, MLA_ROPE)
    qa, ka, vat, qb, kb, vbt = _in_proj(
        x2, mod3, w_main, wvt, wuq_r, w_uk.astype(BF16), w_uv.astype(BF16).T,
        g_q_lat.reshape(1, -1), g_kv_lat.reshape(1, -1), tabs_a, tabs_b, seq)

    bias_mix, bias_causal = _bias_tables(TQ_ATTN, seq)
    oa = _attention(qa, ka, vat, bias_mix, seq=seq, heads=A_HEADS, ek=HEAD_DIM, bias_all_keys=True)
    ob = _attention(qb, kb, vbt, bias_causal, seq=seq, heads=B_HEADS, ek=MLA_QK_PAD, bias_all_keys=False)

    sk = jnp.stack([sub_key_1, sub_key_2]).astype(BF16)
    x1, h2, st = _post_attn(oa, ob, x2, mod3, g_out_a.reshape(1, -1), g_out_b.reshape(1, -1),
                            w_o.astype(BF16), ln1_g.reshape(1, -1), ln1_b.reshape(1, -1),
                            w_pq.astype(BF16), sk, seq)
    e1, cnt, e2, r2 = _peer_topk(st)
    ffn_t = _peer_dense(h2, e1, cnt, e2, r2, u_table.astype(F8), v_table.astype(F8).T)
    return _final_ln(x1, ffn_t, mod3, ln2_g.reshape(1, d), ln2_b.reshape(1, d), seq)


def kernel(x, c, w_ada, b_ada, w_in, g_q_lat, g_kv_lat, w_uq, w_uk, w_uv, g_out_a, g_out_b, w_o, ln1_g, ln1_b,
           w_pq, sub_key_1, sub_key_2, u_table, v_table, ln2_g, ln2_b):
    bsz, seq, d = x.shape
    x2 = x.reshape(bsz * seq, d)
    for l in range(w_ada.shape[0]):
        mod3 = _ada_mod(c, w_ada[l], b_ada[l]).reshape(bsz * 6, 1, d)
        x2 = _layer(x2, mod3, seq, w_in[l], g_q_lat[l], g_kv_lat[l], w_uq[l], w_uk[l], w_uv[l],
                    g_out_a[l], g_out_b[l], w_o[l], ln1_g[l], ln1_b[l], w_pq[l], sub_key_1[l], sub_key_2[l],
                    u_table[l], v_table[l], ln2_g[l], ln2_b[l])
    return x2.reshape(bsz, seq, d)
```

```python
import functools
import math

import jax
import jax.numpy as jnp
from jax import lax
from jax.experimental import pallas as pl
from jax.experimental.pallas import tpu as pltpu

F32 = jnp.float32
BF16 = jnp.bfloat16
F8 = jnp.float8_e4m3fn

D_MODEL = 2048
HEAD_DIM = 128
A_HEADS = 8
A_WIDTH = A_HEADS * HEAD_DIM
A_ROT_DIM = 32
ROPE_THETA = 500000.0
B_HEADS = 8
MLA_RANK = 512
MLA_NOPE = 128
MLA_ROPE = 64
MLA_V = 128
B_WIDTH = B_HEADS * MLA_V
MLA_QK_PAD = 256
PEER_HEADS = 8
PEER_N_KEYS = 128
PEER_TOPK = 16
DN_ALPHA = 2.0 ** 0.25

V7X_LANES = 128
V7X_SUBLANES = 8
V7X_BF16_SUBLANES = 16
V7X_VMEM_LIMIT = 56 * 1024 * 1024

TM_PROJ = 256
TQ_ATTN = 512
TL_TOPK = 256
TM_PEER = 1024
TS_PEER = 512
TE_PEER = 1024
PIECE_PEER = 256
TM_FINAL = 512

NEG_BIG = -1e30
LOG2E = math.log2(math.e)

_NT = (((1,), (1,)), ((), ()))

_C_QA, _C_KA, _C_CQ, _C_CKV, _C_KR = 0, A_WIDTH, 2 * A_WIDTH, 2 * A_WIDTH + MLA_RANK, 2 * A_WIDTH + 2 * MLA_RANK


def _params(*sem):
    return pltpu.CompilerParams(dimension_semantics=sem, vmem_limit_bytes=V7X_VMEM_LIMIT)


def _resident(shape):
    nd = len(shape)
    return pl.BlockSpec(shape, lambda *_: (0,) * nd, pipeline_mode=pl.Buffered(1))


def _layer_norm(x, eps=1e-5):
    mu = jnp.mean(x, axis=-1, keepdims=True)
    xc = x - mu
    var = jnp.mean(xc * xc, axis=-1, keepdims=True)
    return xc * lax.rsqrt(var + eps)


def _rms_norm(x, g, eps=1e-6):
    return x * lax.rsqrt(jnp.mean(x * x, axis=-1, keepdims=True) + eps) * g


def _rope_lanes(x, cos, sin_lo, sin_hi, half):
    return (x * cos + pltpu.roll(x, V7X_LANES - half, 1) * sin_lo + pltpu.roll(x, half, 1) * sin_hi)


def _ada_kernel(c_ref, w_ref, b_ref, o_ref):
    c = c_ref[...]
    act = (c / (1.0 + jnp.exp(-c))).astype(BF16)
    o_ref[...] = jnp.dot(act, w_ref[...].astype(BF16), preferred_element_type=F32) + b_ref[...]


def _ada_mod(c, w_ada, b_ada):
    bsz, d = c.shape
    n = w_ada.shape[1]
    tn = 1024
    return pl.pallas_call(
        _ada_kernel,
        grid=(n // tn,),
        in_specs=[pl.BlockSpec((bsz, d), lambda j: (0, 0)),
                  pl.BlockSpec((d, tn), lambda j: (0, j)),
                  pl.BlockSpec((1, tn), lambda j: (0, j))],
        out_specs=pl.BlockSpec((bsz, tn), lambda j: (0, j)),
        out_shape=jax.ShapeDtypeStruct((bsz, n), F32),
        compiler_params=_params("arbitrary"),
        name="ada_mod",
    )(c, w_ada, b_ada.reshape(1, n))


def _in_proj_kernel(x_ref, sc_ref, sh_ref, w_ref, wvt_ref, wuq_ref, wuk_ref, wuvt_ref, gq_ref, gkv_ref,
                    ca_ref, sal_ref, sah_ref, cb_ref, sbl_ref, sbh_ref,
                    qa_ref, ka_ref, vat_ref, qb_ref, kb_ref, vbt_ref, *, scale_a, scale_b):
    h = (_layer_norm(x_ref[...]) * (1.0 + sc_ref[0]) + sh_ref[0]).astype(BF16)

    def proj(c0, width):
        return jnp.dot(h, w_ref[:, c0:c0 + width], preferred_element_type=F32)

    ca, sal, sah = ca_ref[...], sal_ref[...], sah_ref[...]
    cb, sbl, sbh = cb_ref[...], sbl_ref[...], sbh_ref[...]
    chunk = 512
    for c_base, out_ref, scale in ((_C_QA, qa_ref, scale_a), (_C_KA, ka_ref, None)):
        for cc in range(A_WIDTH // chunk):
            acc = proj(c_base + cc * chunk, chunk)
            for hh in range(chunk // HEAD_DIM):
                r = _rope_lanes(acc[:, hh * HEAD_DIM:(hh + 1) * HEAD_DIM], ca, sal, sah, A_ROT_DIM // 2)
                if scale is not None:
                    r = r * scale
                c0 = cc * chunk + hh * HEAD_DIM
                out_ref[:, c0:c0 + HEAD_DIM] = r.astype(BF16)
    vat_ref[...] = lax.dot_general(wvt_ref[...], h, _NT, preferred_element_type=F32).astype(BF16)
    cq = _rms_norm(proj(_C_CQ, MLA_RANK), gq_ref[...]).astype(BF16)
    ckv = _rms_norm(proj(_C_CKV, MLA_RANK), gkv_ref[...]).astype(BF16)
    kr = _rope_lanes(proj(_C_KR, V7X_LANES), cb, sbl, sbh, MLA_ROPE // 2).astype(BF16)
    for cc in range(B_HEADS * MLA_QK_PAD // chunk):
        qq = jnp.dot(cq, wuq_ref[:, cc * chunk:(cc + 1) * chunk], preferred_element_type=F32)
        for hh in range(chunk // MLA_QK_PAD):
            c0 = cc * chunk + hh * MLA_QK_PAD
            nope = qq[:, hh * MLA_QK_PAD:hh * MLA_QK_PAD + MLA_NOPE]
            rp = qq[:, hh * MLA_QK_PAD + MLA_NOPE:(hh + 1) * MLA_QK_PAD]
            qb_ref[:, c0:c0 + MLA_NOPE] = (nope * scale_b).astype(BF16)
            qb_ref[:, c0 + MLA_NOPE:c0 + MLA_QK_PAD] = (
                _rope_lanes(rp, cb, sbl, sbh, MLA_ROPE // 2) * scale_b).astype(BF16)
    for cc in range(B_WIDTH // chunk):
        kn = jnp.dot(ckv, wuk_ref[:, cc * chunk:(cc + 1) * chunk], preferred_element_type=F32)
        for hh in range(chunk // MLA_NOPE):
            head = cc * (chunk // MLA_NOPE) + hh
            kb_ref[:, head * MLA_QK_PAD:head * MLA_QK_PAD + MLA_NOPE] = (
                kn[:, hh * MLA_NOPE:(hh + 1) * MLA_NOPE].astype(BF16))
            kb_ref[:, head * MLA_QK_PAD + MLA_NOPE:(head + 1) * MLA_QK_PAD] = kr
    vbt_ref[...] = lax.dot_general(wuvt_ref[...], ckv, _NT, preferred_element_type=F32).astype(BF16)


def _in_proj(x2, mod3, w_main, wvt, wuq_r, wuk, wuvt, gq, gkv, tabs_a, tabs_b, seq):
    t, d = x2.shape
    tm = TM_PROJ
    per_b = seq // tm
    row = lambda i: (i, 0)
    col = lambda i: (0, i)
    tab = lambda i: (i % per_b, 0)
    mod_spec = lambda k: pl.BlockSpec((1, 1, d), lambda i: ((i // per_b) * 6 + k, 0, 0))
    qk_b = B_HEADS * MLA_QK_PAD
    out_specs = [pl.BlockSpec((tm, A_WIDTH), row), pl.BlockSpec((tm, A_WIDTH), row),
                 pl.BlockSpec((A_WIDTH, tm), col),
                 pl.BlockSpec((tm, qk_b), row), pl.BlockSpec((tm, qk_b), row),
                 pl.BlockSpec((B_WIDTH, tm), col)]
    out_shape = [jax.ShapeDtypeStruct(s, BF16) for s in
                 ((t, A_WIDTH), (t, A_WIDTH), (A_WIDTH, t), (t, qk_b), (t, qk_b), (B_WIDTH, t))]
    return pl.pallas_call(
        functools.partial(_in_proj_kernel, scale_a=HEAD_DIM ** -0.5 * LOG2E,
                          scale_b=(MLA_NOPE + MLA_ROPE) ** -0.5 * LOG2E),
        grid=(t // tm,),
        in_specs=[pl.BlockSpec((tm, d), row), mod_spec(1), mod_spec(0),
                  _resident(w_main.shape), _resident(wvt.shape), _resident(wuq_r.shape), _resident(wuk.shape),
                  _resident(wuvt.shape), _resident(gq.shape), _resident(gkv.shape)]
                 + [pl.BlockSpec((tm, V7X_LANES), tab)] * 6,
        out_specs=out_specs,
        out_shape=out_shape,
        compiler_params=_params("arbitrary"),
        name="in_proj",
    )(x2, mod3, mod3, w_main, wvt, wuq_r, wuk, wuvt, gq, gkv, *tabs_a, *tabs_b)


def _attn_kernel(q_ref, k_ref, vt_ref, bias_ref, o_ref, *, tq, bias_all_keys):
    seq = q_ref.shape[0]
    nq = seq // tq
    nb = bias_ref.shape[0]

    def key_parts(qi):
        n = (qi + 1) * tq
        if bias_all_keys:
            return [(0, n, True)]
        return ([(0, n - tq, False)] if qi > 0 else []) + [(n - tq, n, True)]

    def score_tiles(qi):
        q = q_ref[qi * tq:(qi + 1) * tq, :]
        tiles = []
        for k0, k1, biased in key_parts(qi):
            st = lax.dot_general(k_ref[k0:k1, :], q, _NT, preferred_element_type=F32)
            tiles.append(st + bias_ref[nb - (k1 - k0):nb, :] if biased else st)
        return tiles

    ahead = score_tiles(0)
    for qi in range(nq):
        parts, scores = key_parts(qi), ahead
        if qi + 1 < nq:
            ahead = score_tiles(qi + 1)
        m = jnp.max(scores[0], axis=0, keepdims=True)
        for st in scores[1:]:
            m = jnp.maximum(m, jnp.max(st, axis=0, keepdims=True))
        l = jnp.zeros_like(m)
        o_t = jnp.zeros((vt_ref.shape[0], tq), F32)
        for (k0, k1, _), st in zip(parts, scores):
            p = jnp.exp2(st - m)
            l = l + jnp.sum(p, axis=0, keepdims=True)
            o_t = o_t + jnp.dot(vt_ref[:, k0:k1], p.astype(BF16), preferred_element_type=F32)
        o_ref[qi * tq:(qi + 1) * tq, :] = (o_t * (1.0 / l)).T


def _attention(q, k, vt, bias, *, seq, heads, ek, bias_all_keys):
    t = q.shape[0]
    bsz = t // seq
    ev = vt.shape[0] // heads
    return pl.pallas_call(
        functools.partial(_attn_kernel, tq=TQ_ATTN, bias_all_keys=bias_all_keys),
        grid=(bsz, heads),
        in_specs=[pl.BlockSpec((seq, ek), lambda b, h: (b, h)),
                  pl.BlockSpec((seq, ek), lambda b, h: (b, h)),
                  pl.BlockSpec((ev, seq), lambda b, h: (h, b)),
                  _resident(bias.shape)],
        out_specs=pl.BlockSpec((seq, ev), lambda b, h: (b, h)),
        out_shape=jax.ShapeDtypeStruct((t, heads * ev), F32),
        compiler_params=_params("arbitrary", "arbitrary"),
        name="attention",
    )(q, k, vt, bias)


def _bias_tables(tq, seq):
    d = ((seq - tq) + jnp.arange(tq, dtype=jnp.int32)[None, :] - jnp.arange(seq, dtype=jnp.int32)[:, None])
    ok = d >= 0
    mult = ((d <= 128).astype(F32) + ((d % 4 == 0) & (d <= 512)).astype(F32)
            + ((d % 16 == 0) & (d <= 2048)).astype(F32))
    mix = jnp.where(ok & (mult > 0), jnp.log2(jnp.maximum(mult, 1.0)), NEG_BIG).astype(F32)
    causal = jnp.where(ok[seq - tq:], 0.0, NEG_BIG).astype(F32)
    return mix, causal


def _post_attn_kernel(oa_ref, ob_ref, x_ref, g1_ref, sc_ref, sh_ref, ga_ref, gb_ref, wo_ref, l1g_ref, l1b_ref,
                      wpq_ref, sk_ref, x1_ref, h2_ref, st_ref):
    na = _rms_norm(oa_ref[...], ga_ref[...]).astype(BF16)
    nb = _rms_norm(ob_ref[...], gb_ref[...]).astype(BF16)
    mix = (jnp.dot(na, wo_ref[0:A_WIDTH, :], preferred_element_type=F32)
           + jnp.dot(nb, wo_ref[A_WIDTH:A_WIDTH + B_WIDTH, :], preferred_element_type=F32))
    x1 = _layer_norm(DN_ALPHA * x_ref[...] + g1_ref[0] * mix) * l1g_ref[...] + l1b_ref[...]
    x1_ref[...] = x1
    h2_f32 = _layer_norm(x1) * (1.0 + sc_ref[0]) + sh_ref[0]
    h2_ref[...] = h2_f32.astype(F8)
    q = jnp.dot(h2_f32.astype(BF16), wpq_ref[...], preferred_element_type=F32).astype(BF16)
    for hh in range(PEER_HEADS):
        for half in range(2):
            r0 = (hh * 2 + half) * PEER_N_KEYS
            st_ref[r0:r0 + PEER_N_KEYS, :] = lax.dot_general(
                sk_ref[half], q[:, r0:r0 + PEER_N_KEYS], _NT, preferred_element_type=F32)


def _post_attn(oa, ob, x2, mod3, ga, gb, wo, l1g, l1b, wpq, sk, seq):
    t, d = x2.shape
    tm = TM_PROJ
    per_b = seq // tm
    row = lambda i: (i, 0)
    mod_spec = lambda k: pl.BlockSpec((1, 1, d), lambda i: ((i // per_b) * 6 + k, 0, 0))
    n_sc = 2 * PEER_HEADS * PEER_N_KEYS
    return pl.pallas_call(
        _post_attn_kernel,
        grid=(t // tm,),
        in_specs=[pl.BlockSpec((tm, A_WIDTH), row), pl.BlockSpec((tm, B_WIDTH), row), pl.BlockSpec((tm, d), row),
                  mod_spec(2), mod_spec(4), mod_spec(3),
                  _resident(ga.shape), _resident(gb.shape), _resident(wo.shape),
                  _resident(l1g.shape), _resident(l1b.shape), _resident(wpq.shape), _resident(sk.shape)],
        out_specs=[pl.BlockSpec((tm, d), row), pl.BlockSpec((tm, d), row),
                   pl.BlockSpec((n_sc, tm), lambda i: (0, i))],
        out_shape=[jax.ShapeDtypeStruct((t, d), F32), jax.ShapeDtypeStruct((t, d), F8),
                   jax.ShapeDtypeStruct((n_sc, t), F32)],
        compiler_params=_params("arbitrary"),
        name="post_attn",
    )(oa, ob, x2, mod3, mod3, mod3, ga, gb, wo, l1g, l1b, wpq, sk)


_GROUPS = PEER_N_KEYS // V7X_SUBLANES


def _oddeven_merge_sort_pairs(n):
    pairs = []
    p = 1
    while p < n:
        k = p
        while k >= 1:
            for j in range(k % p, n - k, 2 * k):
                for i in range(min(k, n - j - k)):
                    if (i + j) // (2 * p) == (i + j + k) // (2 * p):
                        pairs.append((i + j, i + j + k))
            k //= 2
        p *= 2
    return pairs


_SORT16 = _oddeven_merge_sort_pairs(_GROUPS)


def _sublane_max(x):
    for shift in (4, 2, 1):
        x = jnp.maximum(x, pltpu.roll(x, shift, 0))
    return x


def _sorted_top16(groups):
    c = list(groups)
    for i, j in _SORT16:
        c[i], c[j] = jnp.maximum(c[i], c[j]), jnp.minimum(c[i], c[j])
    out = []
    for r in range(PEER_TOPK):
        m = _sublane_max(c[0])
        out.append(m)
        if r + 1 < PEER_TOPK:
            hit = c[0] == m
            c = [jnp.where(hit, c[k + 1], c[k]) for k in range(len(c) - 1)]
    return out


def _count_leading(pred, v):
    p8 = pred(v[7])
    p4 = pred(jnp.where(p8, v[11], v[3]))
    p2 = pred(jnp.where(p8, jnp.where(p4, v[13], v[9]), jnp.where(p4, v[5], v[1])))
    hi = jnp.where(p4, jnp.where(p2, v[14], v[12]), jnp.where(p2, v[10], v[8]))
    lo = jnp.where(p4, jnp.where(p2, v[6], v[4]), jnp.where(p2, v[2], v[0]))
    p1 = pred(jnp.where(p8, hi, lo))
    p16 = pred(v[15])
    one, zero = jnp.float32(1.0), jnp.float32(0.0)
    return (jnp.where(p8, 8.0, zero) + jnp.where(p4, 4.0, zero) + jnp.where(p2, 2.0, zero)
            + jnp.where(p1, one, zero) + jnp.where(p16, one, zero))


def _peer_topk_kernel(st_ref, e1_ref, cnt_ref, e2_ref, r2_ref):
    tl = st_ref.shape[1]
    sub = lax.broadcasted_iota(jnp.int32, (V7X_SUBLANES, tl), 0)

    def by_sublane(rows):
        out = rows[0]
        for j in range(1, V7X_SUBLANES):
            out = jnp.where(sub == j, rows[j], out)
        return out

    for hh in range(PEER_HEADS):
        r0 = hh * 2 * PEER_N_KEYS
        g1 = [st_ref[r0 + V7X_SUBLANES * k:r0 + V7X_SUBLANES * (k + 1), :] for k in range(_GROUPS)]
        g2 = [st_ref[r0 + PEER_N_KEYS + V7X_SUBLANES * k:r0 + PEER_N_KEYS + V7X_SUBLANES * (k + 1), :]
              for k in range(_GROUPS)]
        v1 = _sorted_top16(g1)
        v2 = _sorted_top16(g2)
        v2_lo, v2_hi, v1_hi = by_sublane(v2[:8]), by_sublane(v2[8:]), by_sublane(v1[8:])
        cand = [v1[0] + v2_lo, v1[0] + v2_hi, v1[1] + v2_lo]
        for a in range(2, 8):
            cand.append(jnp.where(sub < PEER_TOPK // (a + 1), v1[a] + v2_lo, -jnp.inf))
        cand.append(v1_hi + v2[0])
        cmax = v1[0] + v2[0]
        z = jnp.zeros_like(cmax)
        tau = cmax
        for r in range(PEER_TOPK):
            tau = cand[0]
            for blk in cand[1:]:
                tau = jnp.maximum(tau, blk)
            tau = _sublane_max(tau)
            z = z + jnp.exp(tau - cmax)
            if r + 1 < PEER_TOPK:
                cand = [jnp.where(blk == tau, -jnp.inf, blk) for blk in cand]
        inv_z = 1.0 / z
        for k in range(_GROUPS):
            rows = slice(hh * PEER_N_KEYS + V7X_SUBLANES * k, hh * PEER_N_KEYS + V7X_SUBLANES * (k + 1))
            x1 = g1[k]
            e1_ref[rows, :] = jnp.exp(x1 - v1[0]) * inv_z
            cnt_ref[rows, :] = _count_leading(lambda vb: x1 + vb >= tau, v2)
        for k in range(0, _GROUPS, 2):
            rows = slice(hh * PEER_N_KEYS + V7X_SUBLANES * k, hh * PEER_N_KEYS + V7X_SUBLANES * (k + 2))
            e2_ref[rows, :] = jnp.concatenate(
                [jnp.exp(g2[k + j] - v2[0]) for j in range(2)], axis=0).astype(BF16)
            r2_ref[rows, :] = jnp.concatenate(
                [_count_leading(lambda vb, x=g2[k + j]: vb > x, v2) for j in range(2)], axis=0).astype(BF16)


def _peer_topk(st):
    rows, t = st.shape
    tl = TL_TOPK
    n = PEER_HEADS * PEER_N_KEYS
    col = lambda i: (0, i)
    return pl.pallas_call(
        _peer_topk_kernel,
        grid=(t // tl,),
        in_specs=[pl.BlockSpec((rows, tl), col)],
        out_specs=[pl.BlockSpec((n, tl), col)] * 4,
        out_shape=[jax.ShapeDtypeStruct((n, t), F32), jax.ShapeDtypeStruct((n, t), F32),
                   jax.ShapeDtypeStruct((n, t), BF16), jax.ShapeDtypeStruct((n, t), BF16)],
        compiler_params=_params("arbitrary"),
        name="peer_topk",
    )(st)


def _gelu(x):
    half = 0.5 * x
    return half + half * lax.erf(x * (2.0 ** -0.5))


def _row_bf16(ref, row, cols, rows_out):
    r = jnp.broadcast_to(ref[pl.ds(row, 1), cols], (V7X_BF16_SUBLANES, cols.stop - cols.start)).astype(BF16)
    return jnp.concatenate([r] * (rows_out // V7X_BF16_SUBLANES), axis=0)


def _peer_dense_kernel(h2_ref, e1_ref, cnt_ref, e2_ref, r2_ref, u_ref, vt_ref, o_ref, a_sc, w_sc, g_sc, *, te, ts):
    e = pl.program_id(1)
    n_i1 = te // PEER_N_KEYS
    tm = h2_ref.shape[0]

    @pl.when(e == 0)
    def _():
        o_ref[...] = jnp.zeros(o_ref.shape, F32)

    chains = [slice(s * ts, (s + 1) * ts) for s in range(tm // ts)]
    d_model = vt_ref.shape[0]
    n_piece = te // PIECE_PEER
    per_piece = PIECE_PEER // PEER_N_KEYS
    d_piece = d_model // n_piece

    def first_piece(c, k):
        rows = slice(k * PIECE_PEER, (k + 1) * PIECE_PEER)
        a_sc[c, rows, :] = lax.dot_general(u_ref[rows, :], h2_ref[chains[c], :], _NT,
                                           preferred_element_type=F32)

    def gate_tile(c, ii):
        i1 = e * n_i1 + ii
        w = jnp.zeros((PEER_N_KEYS, ts), BF16)
        for hh in range(PEER_HEADS):
            rows = slice(hh * PEER_N_KEYS, (hh + 1) * PEER_N_KEYS)
            e1 = _row_bf16(e1_ref, hh * PEER_N_KEYS + i1, chains[c], PEER_N_KEYS)
            cnt = _row_bf16(cnt_ref, hh * PEER_N_KEYS + i1, chains[c], PEER_N_KEYS)
            w = w + jnp.where(r2_ref[rows, chains[c]] < cnt, e2_ref[rows, chains[c]], jnp.zeros_like(w)) * e1
        w_sc[c, ii * PEER_N_KEYS:(ii + 1) * PEER_N_KEYS, :] = w

    def act_tile(c, ii):
        rows = slice(ii * PEER_N_KEYS, (ii + 1) * PEER_N_KEYS)
        g_sc[c, rows, :] = (_gelu(a_sc[c, rows, :].astype(BF16)) * w_sc[c, rows, :]).astype(F8)

    def second_piece(c, k):
        rows = slice(k * d_piece, (k + 1) * d_piece)
        o_ref[rows, chains[c]] += jnp.dot(vt_ref[rows, :], g_sc[c], preferred_element_type=F32)

    def tiles_of(k):
        return range(k * per_piece, (k + 1) * per_piece)

    for c in range(len(chains)):
        for k in range(n_piece):
            first_piece(c, k)
            for ii in tiles_of(k):
                if c == 0:
                    gate_tile(0, ii)
                if c == 1 or len(chains) == 1:
                    act_tile(0, ii)
    for c in range(len(chains)):
        for k in range(n_piece):
            second_piece(c, k)
            if c + 1 < len(chains):
                for ii in tiles_of(k):
                    gate_tile(c + 1, ii)
        if c + 1 < len(chains):
            for ii in range(n_i1):
                act_tile(c + 1, ii)


def _peer_dense(h2, e1, cnt, e2, r2, u_bf, vt_bf):
    t, d = h2.shape
    n_e = u_bf.shape[0]
    n = e1.shape[0]
    tm, te = TM_PEER, TE_PEER
    n_chain = tm // TS_PEER
    tok = lambda i, e: (0, i)
    once = pl.Buffered(1)
    return pl.pallas_call(
        functools.partial(_peer_dense_kernel, te=te, ts=TS_PEER),
        grid=(t // tm, n_e // te),
        in_specs=[pl.BlockSpec((tm, d), lambda i, e: (i, 0)),
                  pl.BlockSpec((n, tm), tok), pl.BlockSpec((n, tm), tok),
                  pl.BlockSpec((n, tm), tok), pl.BlockSpec((n, tm), tok),
                  pl.BlockSpec((te, d), lambda i, e: (e, 0)),
                  pl.BlockSpec((d, te), lambda i, e: (0, e))],
        out_specs=pl.BlockSpec((d, tm), tok, pipeline_mode=once),
        out_shape=jax.ShapeDtypeStruct((d, t), F32),
        scratch_shapes=[pltpu.VMEM((n_chain, te, TS_PEER), F32), pltpu.VMEM((n_chain, te, TS_PEER), BF16),
                        pltpu.VMEM((n_chain, te, TS_PEER), F8)],
        compiler_params=_params("arbitrary", "arbitrary"),
        name="peer_dense",
    )(h2, e1, cnt, e2, r2, u_bf, vt_bf)


def _table_t_kernel(x_ref, o_ref):
    o_ref[...] = x_ref[...].T.astype(o_ref.dtype)


def _table_t(x, dtype):
    rows, cols = x.shape
    tr = 512
    return pl.pallas_call(
        _table_t_kernel,
        grid=(rows // tr,),
        in_specs=[pl.BlockSpec((tr, cols), lambda i: (i, 0))],
        out_specs=pl.BlockSpec((cols, tr), lambda i: (0, i)),
        out_shape=jax.ShapeDtypeStruct((cols, rows), dtype),
        compiler_params=_params("arbitrary"),
        name="table_t",
    )(x)


def _final_ln_kernel(x1_ref, ft_ref, g2_ref, lg_ref, lb_ref, o_ref):
    o_ref[...] = _layer_norm(DN_ALPHA * x1_ref[...] + g2_ref[0] * ft_ref[...].T) * lg_ref[...] + lb_ref[...]


def _final_ln(x1, ffn_t, mod3, lg, lb, seq):
    t, d = x1.shape
    tm = TM_FINAL
    per_b = seq // tm
    row = lambda i: (i, 0)
    return pl.pallas_call(
        _final_ln_kernel,
        grid=(t // tm,),
        in_specs=[pl.BlockSpec((tm, d), row), pl.BlockSpec((d, tm), lambda i: (0, i)),
                  pl.BlockSpec((1, 1, d), lambda i: ((i // per_b) * 6 + 5, 0, 0)),
                  _resident(lg.shape), _resident(lb.shape)],
        out_specs=pl.BlockSpec((tm, d), row),
        out_shape=jax.ShapeDtypeStruct((t, d), F32),
        compiler_params=_params("arbitrary"),
        name="final_ln",
    )(x1, ffn_t, mod3, lg, lb)


def _rope_tables(seq, rot_dim):
    half = rot_dim // 2
    pos = jnp.arange(seq, dtype=F32)
    inv = ROPE_THETA ** (-jnp.arange(half, dtype=F32) * 2.0 / rot_dim)
    ang = pos[:, None] * inv[None, :]
    cos, sin = jnp.cos(ang), jnp.sin(ang)
    pad = V7X_LANES - rot_dim
    ones = jnp.ones((seq, pad), F32)
    zeros = jnp.zeros((seq, pad), F32)
    zh = jnp.zeros((seq, half), F32)
    return (jnp.concatenate([cos, cos, ones], 1),
            jnp.concatenate([-sin, zh, zeros], 1),
            jnp.concatenate([zh, sin, zeros], 1))


def _layer(x2, mod3, seq, w_in, g_q_lat, g_kv_lat, w_uq, w_uk, w_uv, g_out_a, g_out_b, w_o,
           ln1_g, ln1_b, w_pq, sub_key_1, sub_key_2, u_table, v_table, ln2_g, ln2_b):
    d = x2.shape[1]
    w_bf = w_in.astype(BF16)
    w_main = jnp.concatenate([w_bf[:, :2 * A_WIDTH], w_bf[:, 3 * A_WIDTH:]], axis=1)
    w_main = jnp.pad(w_main, ((0, 0), (0, (-w_main.shape[1]) % V7X_LANES)))
    wvt = w_bf[:, 2 * A_WIDTH:3 * A_WIDTH].T
    pad_q = MLA_QK_PAD - MLA_NOPE - MLA_ROPE
    wuq_r = jnp.pad(w_uq.astype(BF16).reshape(MLA_RANK, B_HEADS, MLA_NOPE + MLA_ROPE),
                    ((0, 0), (0, 0), (0, pad_q))).reshape(MLA_RANK, B_HEADS * MLA_QK_PAD)
    tabs_a = _rope_tables(seq, A_ROT_DIM)
    tabs_b = _rope_tables(seq, MLA_ROPE)
    qa, ka, vat, qb, kb, vbt = _in_proj(
        x2, mod3, w_main, wvt, wuq_r, w_uk.astype(BF16), w_uv.astype(BF16).T,
        g_q_lat.reshape(1, -1), g_kv_lat.reshape(1, -1), tabs_a, tabs_b, seq)

    bias_mix, bias_causal = _bias_tables(TQ_ATTN, seq)
    oa = _attention(qa, ka, vat, bias_mix, seq=seq, heads=A_HEADS, ek=HEAD_DIM, bias_all_keys=True)
    ob = _attention(qb, kb, vbt, bias_causal, seq=seq, heads=B_HEADS, ek=MLA_QK_PAD, bias_all_keys=False)

    sk = jnp.stack([sub_key_1, sub_key_2]).astype(BF16)
    x1, h2, st = _post_attn(oa, ob, x2, mod3, g_out_a.reshape(1, -1), g_out_b.reshape(1, -1),
                            w_o.astype(BF16), ln1_g.reshape(1, -1), ln1_b.reshape(1, -1),
                            w_pq.astype(BF16), sk, seq)
    e1, cnt, e2, r2 = _peer_topk(st)
    ffn_t = _peer_dense(h2, e1, cnt, e2, r2, u_table.astype(F8), _table_t(v_table, F8))
    return _final_ln(x1, ffn_t, mod3, ln2_g.reshape(1, d), ln2_b.reshape(1, d), seq)


def kernel(x, c, w_ada, b_ada, w_in, g_q_lat, g_kv_lat, w_uq, w_uk, w_uv, g_out_a, g_out_b, w_o, ln1_g, ln1_b,
           w_pq, sub_key_1, sub_key_2, u_table, v_table, ln2_g, ln2_b):
    bsz, seq, d = x.shape
    x2 = x.reshape(bsz * seq, d)
    for l in range(w_ada.shape[0]):
        mod3 = _ada_mod(c, w_ada[l], b_ada[l]).reshape(bsz * 6, 1, d)
        x2 = _layer(x2, mod3, seq, w_in[l], g_q_lat[l], g_kv_lat[l], w_uq[l], w_uk[l], w_uv[l],
                    g_out_a[l], g_out_b[l], w_o[l], ln1_g[l], ln1_b[l], w_pq[l], sub_key_1[l], sub_key_2[l],
                    u_table[l], v_table[l], ln2_g[l], ln2_b[l])
    return x2.reshape(bsz, seq, d)
```

```python
import functools
import math

import jax
import jax.numpy as jnp
from jax import lax
from jax.experimental import pallas as pl
from jax.experimental.pallas import tpu as pltpu

F32 = jnp.float32
BF16 = jnp.bfloat16
F8 = jnp.float8_e4m3fn
F8_MAX = 448.0

D_MODEL = 2048
HEAD_DIM = 128
A_HEADS = 8
A_WIDTH = A_HEADS * HEAD_DIM
A_ROT_DIM = 32
ROPE_THETA = 500000.0
B_HEADS = 8
MLA_RANK = 512
MLA_NOPE = 128
MLA_ROPE = 64
MLA_V = 128
B_WIDTH = B_HEADS * MLA_V
MLA_QK_PAD = 256
PEER_HEADS = 8
PEER_N_KEYS = 128
PEER_TOPK = 16
DN_ALPHA = 2.0 ** 0.25

V7X_LANES = 128
V7X_SUBLANES = 8
V7X_BF16_SUBLANES = 16
V7X_VMEM_LIMIT = 56 * 1024 * 1024

TM_PROJ = 256
TQ_ATTN = 512
TL_TOPK = 256
TM_PEER = 1024
TS_PEER = 512
TE_PEER = 1024
PIECE_PEER = 256
TM_FINAL = 512

NEG_BIG = -1e30
LOG2E = math.log2(math.e)

_NT = (((1,), (1,)), ((), ()))

_C_QA, _C_KA, _C_CQ, _C_CKV, _C_KR = 0, A_WIDTH, 2 * A_WIDTH, 2 * A_WIDTH + MLA_RANK, 2 * A_WIDTH + 2 * MLA_RANK


def _params(*sem):
    return pltpu.CompilerParams(dimension_semantics=sem, vmem_limit_bytes=V7X_VMEM_LIMIT)


def _resident(shape):
    nd = len(shape)
    return pl.BlockSpec(shape, lambda *_: (0,) * nd, pipeline_mode=pl.Buffered(1))


def _layer_norm(x, eps=1e-5):
    mu = jnp.mean(x, axis=-1, keepdims=True)
    xc = x - mu
    var = jnp.mean(xc * xc, axis=-1, keepdims=True)
    return xc * lax.rsqrt(var + eps)


def _rms_norm(x, g, eps=1e-6):
    return x * lax.rsqrt(jnp.mean(x * x, axis=-1, keepdims=True) + eps) * g


def _to_f8(x):
    big = jnp.asarray(F8_MAX, x.dtype)
    return lax.clamp(-big, x, big).astype(F8)


def _rope_lanes(x, cos, sin_lo, sin_hi, half):
    return (x * cos + pltpu.roll(x, V7X_LANES - half, 1) * sin_lo + pltpu.roll(x, half, 1) * sin_hi)


def _ada_kernel(c_ref, w_ref, b_ref, o_ref):
    c = c_ref[...]
    act = (c / (1.0 + jnp.exp(-c))).astype(BF16)
    o_ref[...] = jnp.dot(act, w_ref[...].astype(BF16), preferred_element_type=F32) + b_ref[...]


def _ada_mod(c, w_ada, b_ada):
    bsz, d = c.shape
    n = w_ada.shape[1]
    tn = 1024
    return pl.pallas_call(
        _ada_kernel,
        grid=(n // tn,),
        in_specs=[pl.BlockSpec((bsz, d), lambda j: (0, 0)),
                  pl.BlockSpec((d, tn), lambda j: (0, j)),
                  pl.BlockSpec((1, tn), lambda j: (0, j))],
        out_specs=pl.BlockSpec((bsz, tn), lambda j: (0, j)),
        out_shape=jax.ShapeDtypeStruct((bsz, n), F32),
        compiler_params=_params("arbitrary"),
        name="ada_mod",
    )(c, w_ada, b_ada.reshape(1, n))


def _in_proj_kernel(x_ref, sc_ref, sh_ref, w_ref, wvt_ref, wuq_ref, wuk_ref, wuvt_ref, gq_ref, gkv_ref,
                    ca_ref, sal_ref, sah_ref, cb_ref, sbl_ref, sbh_ref,
                    qa_ref, ka_ref, vat_ref, qb_ref, kb_ref, vbt_ref, *, scale_a, scale_b):
    h = (_layer_norm(x_ref[...]) * (1.0 + sc_ref[0]) + sh_ref[0]).astype(BF16)

    def proj(c0, width):
        return jnp.dot(h, w_ref[:, c0:c0 + width], preferred_element_type=F32)

    ca, sal, sah = ca_ref[...], sal_ref[...], sah_ref[...]
    cb, sbl, sbh = cb_ref[...], sbl_ref[...], sbh_ref[...]
    chunk = 512
    for c_base, out_ref, scale in ((_C_QA, qa_ref, scale_a), (_C_KA, ka_ref, None)):
        for cc in range(A_WIDTH // chunk):
            acc = proj(c_base + cc * chunk, chunk)
            for hh in range(chunk // HEAD_DIM):
                r = _rope_lanes(acc[:, hh * HEAD_DIM:(hh + 1) * HEAD_DIM], ca, sal, sah, A_ROT_DIM // 2)
                if scale is not None:
                    r = r * scale
                c0 = cc * chunk + hh * HEAD_DIM
                out_ref[:, c0:c0 + HEAD_DIM] = r.astype(BF16)
    vat_ref[...] = lax.dot_general(wvt_ref[...], h, _NT, preferred_element_type=F32).astype(BF16)
    cq = _rms_norm(proj(_C_CQ, MLA_RANK), gq_ref[...]).astype(BF16)
    ckv = _rms_norm(proj(_C_CKV, MLA_RANK), gkv_ref[...]).astype(BF16)
    kr = _rope_lanes(proj(_C_KR, V7X_LANES), cb, sbl, sbh, MLA_ROPE // 2).astype(BF16)
    for cc in range(B_HEADS * MLA_QK_PAD // chunk):
        qq = jnp.dot(cq, wuq_ref[:, cc * chunk:(cc + 1) * chunk], preferred_element_type=F32)
        for hh in range(chunk // MLA_QK_PAD):
            c0 = cc * chunk + hh * MLA_QK_PAD
            nope = qq[:, hh * MLA_QK_PAD:hh * MLA_QK_PAD + MLA_NOPE]
            rp = qq[:, hh * MLA_QK_PAD + MLA_NOPE:(hh + 1) * MLA_QK_PAD]
            qb_ref[:, c0:c0 + MLA_NOPE] = (nope * scale_b).astype(BF16)
            qb_ref[:, c0 + MLA_NOPE:c0 + MLA_QK_PAD] = (
                _rope_lanes(rp, cb, sbl, sbh, MLA_ROPE // 2) * scale_b).astype(BF16)
    for cc in range(B_WIDTH // chunk):
        kn = jnp.dot(ckv, wuk_ref[:, cc * chunk:(cc + 1) * chunk], preferred_element_type=F32)
        for hh in range(chunk // MLA_NOPE):
            head = cc * (chunk // MLA_NOPE) + hh
            kb_ref[:, head * MLA_QK_PAD:head * MLA_QK_PAD + MLA_NOPE] = (
                kn[:, hh * MLA_NOPE:(hh + 1) * MLA_NOPE].astype(BF16))
            kb_ref[:, head * MLA_QK_PAD + MLA_NOPE:(head + 1) * MLA_QK_PAD] = kr
    vbt_ref[...] = lax.dot_general(wuvt_ref[...], ckv, _NT, preferred_element_type=F32).astype(BF16)


def _in_proj(x2, mod3, w_main, wvt, wuq_r, wuk, wuvt, gq, gkv, tabs_a, tabs_b, seq):
    t, d = x2.shape
    tm = TM_PROJ
    per_b = seq // tm
    row = lambda i: (i, 0)
    col = lambda i: (0, i)
    tab = lambda i: (i % per_b, 0)
    mod_spec = lambda k: pl.BlockSpec((1, 1, d), lambda i: ((i // per_b) * 6 + k, 0, 0))
    qk_b = B_HEADS * MLA_QK_PAD
    out_specs = [pl.BlockSpec((tm, A_WIDTH), row), pl.BlockSpec((tm, A_WIDTH), row),
                 pl.BlockSpec((A_WIDTH, tm), col),
                 pl.BlockSpec((tm, qk_b), row), pl.BlockSpec((tm, qk_b), row),
                 pl.BlockSpec((B_WIDTH, tm), col)]
    out_shape = [jax.ShapeDtypeStruct(s, BF16) for s in
                 ((t, A_WIDTH), (t, A_WIDTH), (A_WIDTH, t), (t, qk_b), (t, qk_b), (B_WIDTH, t))]
    return pl.pallas_call(
        functools.partial(_in_proj_kernel, scale_a=HEAD_DIM ** -0.5 * LOG2E,
                          scale_b=(MLA_NOPE + MLA_ROPE) ** -0.5 * LOG2E),
        grid=(t // tm,),
        in_specs=[pl.BlockSpec((tm, d), row), mod_spec(1), mod_spec(0),
                  _resident(w_main.shape), _resident(wvt.shape), _resident(wuq_r.shape), _resident(wuk.shape),
                  _resident(wuvt.shape), _resident(gq.shape), _resident(gkv.shape)]
                 + [pl.BlockSpec((tm, V7X_LANES), tab)] * 6,
        out_specs=out_specs,
        out_shape=out_shape,
        compiler_params=_params("arbitrary"),
        name="in_proj",
    )(x2, mod3, mod3, w_main, wvt, wuq_r, wuk, wuvt, gq, gkv, *tabs_a, *tabs_b)


def _attn_kernel(q_ref, k_ref, vt_ref, bias_ref, o_ref, *, tq, bias_all_keys):
    seq = q_ref.shape[0]
    nq = seq // tq
    nb = bias_ref.shape[0]

    def key_parts(qi):
        n = (qi + 1) * tq
        if bias_all_keys:
            return [(0, n, True)]
        return ([(0, n - tq, False)] if qi > 0 else []) + [(n - tq, n, True)]

    def score_tiles(qi):
        q = q_ref[qi * tq:(qi + 1) * tq, :]
        tiles = []
        for k0, k1, biased in key_parts(qi):
            st = lax.dot_general(k_ref[k0:k1, :], q, _NT, preferred_element_type=F32)
            tiles.append(st + bias_ref[nb - (k1 - k0):nb, :] if biased else st)
        return tiles

    ahead = score_tiles(0)
    for qi in range(nq):
        parts, scores = key_parts(qi), ahead
        if qi + 1 < nq:
            ahead = score_tiles(qi + 1)
        m = jnp.max(scores[0], axis=0, keepdims=True)
        for st in scores[1:]:
            m = jnp.maximum(m, jnp.max(st, axis=0, keepdims=True))
        l = jnp.zeros_like(m)
        o_t = jnp.zeros((vt_ref.shape[0], tq), F32)
        for (k0, k1, _), st in zip(parts, scores):
            p = jnp.exp2(st - m)
            l = l + jnp.sum(p, axis=0, keepdims=True)
            o_t = o_t + jnp.dot(vt_ref[:, k0:k1], p.astype(BF16), preferred_element_type=F32)
        o_ref[qi * tq:(qi + 1) * tq, :] = (o_t * (1.0 / l)).T


def _attention(q, k, vt, bias, *, seq, heads, ek, bias_all_keys):
    t = q.shape[0]
    bsz = t // seq
    ev = vt.shape[0] // heads
    return pl.pallas_call(
        functools.partial(_attn_kernel, tq=TQ_ATTN, bias_all_keys=bias_all_keys),
        grid=(bsz, heads),
        in_specs=[pl.BlockSpec((seq, ek), lambda b, h: (b, h)),
                  pl.BlockSpec((seq, ek), lambda b, h: (b, h)),
                  pl.BlockSpec((ev, seq), lambda b, h: (h, b)),
                  _resident(bias.shape)],
        out_specs=pl.BlockSpec((seq, ev), lambda b, h: (b, h)),
        out_shape=jax.ShapeDtypeStruct((t, heads * ev), F32),
        compiler_params=_params("arbitrary", "arbitrary"),
        name="attention",
    )(q, k, vt, bias)


def _bias_tables(tq, seq):
    d = ((seq - tq) + jnp.arange(tq, dtype=jnp.int32)[None, :] - jnp.arange(seq, dtype=jnp.int32)[:, None])
    ok = d >= 0
    mult = ((d <= 128).astype(F32) + ((d % 4 == 0) & (d <= 512)).astype(F32)
            + ((d % 16 == 0) & (d <= 2048)).astype(F32))
    mix = jnp.where(ok & (mult > 0), jnp.log2(jnp.maximum(mult, 1.0)), NEG_BIG).astype(F32)
    causal = jnp.where(ok[seq - tq:], 0.0, NEG_BIG).astype(F32)
    return mix, causal


def _post_attn_kernel(oa_ref, ob_ref, x_ref, g1_ref, sc_ref, sh_ref, ga_ref, gb_ref, wo_ref, l1g_ref, l1b_ref,
                      wpq_ref, sk_ref, x1_ref, h2_ref, st_ref):
    na = _rms_norm(oa_ref[...], ga_ref[...]).astype(BF16)
    nb = _rms_norm(ob_ref[...], gb_ref[...]).astype(BF16)
    mix = (jnp.dot(na, wo_ref[0:A_WIDTH, :], preferred_element_type=F32)
           + jnp.dot(nb, wo_ref[A_WIDTH:A_WIDTH + B_WIDTH, :], preferred_element_type=F32))
    x1 = _layer_norm(DN_ALPHA * x_ref[...] + g1_ref[0] * mix) * l1g_ref[...] + l1b_ref[...]
    x1_ref[...] = x1
    h2_f32 = _layer_norm(x1) * (1.0 + sc_ref[0]) + sh_ref[0]
    h2_ref[...] = _to_f8(h2_f32)
    q = jnp.dot(h2_f32.astype(BF16), wpq_ref[...], preferred_element_type=F32).astype(BF16)
    for hh in range(PEER_HEADS):
        for half in range(2):
            r0 = (hh * 2 + half) * PEER_N_KEYS
            st_ref[r0:r0 + PEER_N_KEYS, :] = lax.dot_general(
                sk_ref[half], q[:, r0:r0 + PEER_N_KEYS], _NT, preferred_element_type=F32)


def _post_attn(oa, ob, x2, mod3, ga, gb, wo, l1g, l1b, wpq, sk, seq):
    t, d = x2.shape
    tm = TM_PROJ
    per_b = seq // tm
    row = lambda i: (i, 0)
    mod_spec = lambda k: pl.BlockSpec((1, 1, d), lambda i: ((i // per_b) * 6 + k, 0, 0))
    n_sc = 2 * PEER_HEADS * PEER_N_KEYS
    return pl.pallas_call(
        _post_attn_kernel,
        grid=(t // tm,),
        in_specs=[pl.BlockSpec((tm, A_WIDTH), row), pl.BlockSpec((tm, B_WIDTH), row), pl.BlockSpec((tm, d), row),
                  mod_spec(2), mod_spec(4), mod_spec(3),
                  _resident(ga.shape), _resident(gb.shape), _resident(wo.shape),
                  _resident(l1g.shape), _resident(l1b.shape), _resident(wpq.shape), _resident(sk.shape)],
        out_specs=[pl.BlockSpec((tm, d), row), pl.BlockSpec((tm, d), row),
                   pl.BlockSpec((n_sc, tm), lambda i: (0, i))],
        out_shape=[jax.ShapeDtypeStruct((t, d), F32), jax.ShapeDtypeStruct((t, d), F8),
                   jax.ShapeDtypeStruct((n_sc, t), F32)],
        compiler_params=_params("arbitrary"),
        name="post_attn",
    )(oa, ob, x2, mod3, mod3, mod3, ga, gb, wo, l1g, l1b, wpq, sk)


_GROUPS = PEER_N_KEYS // V7X_SUBLANES


def _oddeven_merge_sort_pairs(n):
    pairs = []
    p = 1
    while p < n:
        k = p
        while k >= 1:
            for j in range(k % p, n - k, 2 * k):
                for i in range(min(k, n - j - k)):
                    if (i + j) // (2 * p) == (i + j + k) // (2 * p):
                        pairs.append((i + j, i + j + k))
            k //= 2
        p *= 2
    return pairs


_SORT16 = _oddeven_merge_sort_pairs(_GROUPS)


def _sublane_max(x):
    for shift in (4, 2, 1):
        x = jnp.maximum(x, pltpu.roll(x, shift, 0))
    return x


def _sorted_top16(groups):
    c = list(groups)
    for i, j in _SORT16:
        c[i], c[j] = jnp.maximum(c[i], c[j]), jnp.minimum(c[i], c[j])
    out = []
    for r in range(PEER_TOPK):
        m = _sublane_max(c[0])
        out.append(m)
        if r + 1 < PEER_TOPK:
            hit = c[0] == m
            c = [jnp.where(hit, c[k + 1], c[k]) for k in range(len(c) - 1)]
    return out


def _count_leading(pred, v):
    p8 = pred(v[7])
    p4 = pred(jnp.where(p8, v[11], v[3]))
    p2 = pred(jnp.where(p8, jnp.where(p4, v[13], v[9]), jnp.where(p4, v[5], v[1])))
    hi = jnp.where(p4, jnp.where(p2, v[14], v[12]), jnp.where(p2, v[10], v[8]))
    lo = jnp.where(p4, jnp.where(p2, v[6], v[4]), jnp.where(p2, v[2], v[0]))
    p1 = pred(jnp.where(p8, hi, lo))
    p16 = pred(v[15])
    one, zero = jnp.float32(1.0), jnp.float32(0.0)
    return (jnp.where(p8, 8.0, zero) + jnp.where(p4, 4.0, zero) + jnp.where(p2, 2.0, zero)
            + jnp.where(p1, one, zero) + jnp.where(p16, one, zero))


def _peer_topk_kernel(st_ref, e1_ref, cnt_ref, e2_ref, r2_ref):
    tl = st_ref.shape[1]
    sub = lax.broadcasted_iota(jnp.int32, (V7X_SUBLANES, tl), 0)

    def by_sublane(rows):
        out = rows[0]
        for j in range(1, V7X_SUBLANES):
            out = jnp.where(sub == j, rows[j], out)
        return out

    for hh in range(PEER_HEADS):
        r0 = hh * 2 * PEER_N_KEYS
        g1 = [st_ref[r0 + V7X_SUBLANES * k:r0 + V7X_SUBLANES * (k + 1), :] for k in range(_GROUPS)]
        g2 = [st_ref[r0 + PEER_N_KEYS + V7X_SUBLANES * k:r0 + PEER_N_KEYS + V7X_SUBLANES * (k + 1), :]
              for k in range(_GROUPS)]
        v1 = _sorted_top16(g1)
        v2 = _sorted_top16(g2)
        v2_lo, v2_hi, v1_hi = by_sublane(v2[:8]), by_sublane(v2[8:]), by_sublane(v1[8:])
        cand = [v1[0] + v2_lo, v1[0] + v2_hi, v1[1] + v2_lo]
        for a in range(2, 8):
            cand.append(jnp.where(sub < PEER_TOPK // (a + 1), v1[a] + v2_lo, -jnp.inf))
        cand.append(v1_hi + v2[0])
        cmax = v1[0] + v2[0]
        z = jnp.zeros_like(cmax)
        tau = cmax
        for r in range(PEER_TOPK):
            tau = cand[0]
            for blk in cand[1:]:
                tau = jnp.maximum(tau, blk)
            tau = _sublane_max(tau)
            z = z + jnp.exp(tau - cmax)
            if r + 1 < PEER_TOPK:
                cand = [jnp.where(blk == tau, -jnp.inf, blk) for blk in cand]
        inv_z = 1.0 / z
        for k in range(_GROUPS):
            rows = slice(hh * PEER_N_KEYS + V7X_SUBLANES * k, hh * PEER_N_KEYS + V7X_SUBLANES * (k + 1))
            x1 = g1[k]
            e1_ref[rows, :] = jnp.exp(x1 - v1[0]) * inv_z
            cnt_ref[rows, :] = _count_leading(lambda vb: x1 + vb >= tau, v2)
        for k in range(0, _GROUPS, 2):
            rows = slice(hh * PEER_N_KEYS + V7X_SUBLANES * k, hh * PEER_N_KEYS + V7X_SUBLANES * (k + 2))
            e2_ref[rows, :] = jnp.concatenate(
                [jnp.exp(g2[k + j] - v2[0]) for j in range(2)], axis=0).astype(BF16)
            r2_ref[rows, :] = jnp.concatenate(
                [_count_leading(lambda vb, x=g2[k + j]: vb > x, v2) for j in range(2)], axis=0).astype(BF16)


def _peer_topk(st):
    rows, t = st.shape
    tl = TL_TOPK
    n = PEER_HEADS * PEER_N_KEYS
    col = lambda i: (0, i)
    return pl.pallas_call(
        _peer_topk_kernel,
        grid=(t // tl,),
        in_specs=[pl.BlockSpec((rows, tl), col)],
        out_specs=[pl.BlockSpec((n, tl), col)] * 4,
        out_shape=[jax.ShapeDtypeStruct((n, t), F32), jax.ShapeDtypeStruct((n, t), F32),
                   jax.ShapeDtypeStruct((n, t), BF16), jax.ShapeDtypeStruct((n, t), BF16)],
        compiler_params=_params("arbitrary"),
        name="peer_topk",
    )(st)


def _gelu(x):
    half = 0.5 * x
    return half + half * lax.erf(x * (2.0 ** -0.5))


def _row_bf16(ref, row, cols, rows_out):
    r = jnp.broadcast_to(ref[pl.ds(row, 1), cols], (V7X_BF16_SUBLANES, cols.stop - cols.start)).astype(BF16)
    return jnp.concatenate([r] * (rows_out // V7X_BF16_SUBLANES), axis=0)


def _peer_dense_kernel(h2_ref, e1_ref, cnt_ref, e2_ref, r2_ref, u_ref, vt_ref, o_ref, a_sc, w_sc, g_sc, *, te, ts):
    e = pl.program_id(1)
    n_i1 = te // PEER_N_KEYS
    tm = h2_ref.shape[0]

    @pl.when(e == 0)
    def _():
        o_ref[...] = jnp.zeros(o_ref.shape, F32)

    chains = [slice(s * ts, (s + 1) * ts) for s in range(tm // ts)]
    d_model = vt_ref.shape[0]
    n_piece = te // PIECE_PEER
    per_piece = PIECE_PEER // PEER_N_KEYS
    d_piece = d_model // n_piece

    def first_piece(c, k):
        rows = slice(k * PIECE_PEER, (k + 1) * PIECE_PEER)
        a_sc[c, rows, :] = lax.dot_general(u_ref[rows, :], h2_ref[chains[c], :], _NT,
                                           preferred_element_type=F32)

    def gate_tile(c, ii):
        i1 = e * n_i1 + ii
        w = jnp.zeros((PEER_N_KEYS, ts), BF16)
        for hh in range(PEER_HEADS):
            rows = slice(hh * PEER_N_KEYS, (hh + 1) * PEER_N_KEYS)
            e1 = _row_bf16(e1_ref, hh * PEER_N_KEYS + i1, chains[c], PEER_N_KEYS)
            cnt = _row_bf16(cnt_ref, hh * PEER_N_KEYS + i1, chains[c], PEER_N_KEYS)
            w = w + jnp.where(r2_ref[rows, chains[c]] < cnt, e2_ref[rows, chains[c]], jnp.zeros_like(w)) * e1
        w_sc[c, ii * PEER_N_KEYS:(ii + 1) * PEER_N_KEYS, :] = w

    def act_tile(c, ii):
        rows = slice(ii * PEER_N_KEYS, (ii + 1) * PEER_N_KEYS)
        g_sc[c, rows, :] = _to_f8(_gelu(a_sc[c, rows, :].astype(BF16)) * w_sc[c, rows, :])

    def second_piece(c, k):
        rows = slice(k * d_piece, (k + 1) * d_piece)
        o_ref[rows, chains[c]] += jnp.dot(vt_ref[rows, :], g_sc[c], preferred_element_type=F32)

    def tiles_of(k):
        return range(k * per_piece, (k + 1) * per_piece)

    for c in range(len(chains)):
        for k in range(n_piece):
            first_piece(c, k)
            for ii in tiles_of(k):
                if c == 0:
                    gate_tile(0, ii)
                if c == 1 or len(chains) == 1:
                    act_tile(0, ii)
    for c in range(len(chains)):
        for k in range(n_piece):
            second_piece(c, k)
            if c + 1 < len(chains):
                for ii in tiles_of(k):
                    gate_tile(c + 1, ii)
        if c + 1 < len(chains):
            for ii in range(n_i1):
                act_tile(c + 1, ii)


def _peer_dense(h2, e1, cnt, e2, r2, u_bf, vt_bf):
    t, d = h2.shape
    n_e = u_bf.shape[0]
    n = e1.shape[0]
    tm, te = TM_PEER, TE_PEER
    n_chain = tm // TS_PEER
    tok = lambda i, e: (0, i)
    once = pl.Buffered(1)
    return pl.pallas_call(
        functools.partial(_peer_dense_kernel, te=te, ts=TS_PEER),
        grid=(t // tm, n_e // te),
        in_specs=[pl.BlockSpec((tm, d), lambda i, e: (i, 0)),
                  pl.BlockSpec((n, tm), tok), pl.BlockSpec((n, tm), tok),
                  pl.BlockSpec((n, tm), tok), pl.BlockSpec((n, tm), tok),
                  pl.BlockSpec((te, d), lambda i, e: (e, 0)),
                  pl.BlockSpec((d, te), lambda i, e: (0, e))],
        out_specs=pl.BlockSpec((d, tm), tok, pipeline_mode=once),
        out_shape=jax.ShapeDtypeStruct((d, t), F32),
        scratch_shapes=[pltpu.VMEM((n_chain, te, TS_PEER), F32), pltpu.VMEM((n_chain, te, TS_PEER), BF16),
                        pltpu.VMEM((n_chain, te, TS_PEER), F8)],
        compiler_params=_params("arbitrary", "arbitrary"),
        name="peer_dense",
    )(h2, e1, cnt, e2, r2, u_bf, vt_bf)


def _table_t_kernel(x_ref, o_ref):
    o_ref[...] = _to_f8(x_ref[...].T)


def _table_t(x):
    rows, cols = x.shape
    tr = 512
    return pl.pallas_call(
        _table_t_kernel,
        grid=(rows // tr,),
        in_specs=[pl.BlockSpec((tr, cols), lambda i: (i, 0))],
        out_specs=pl.BlockSpec((cols, tr), lambda i: (0, i)),
        out_shape=jax.ShapeDtypeStruct((cols, rows), F8),
        compiler_params=_params("arbitrary"),
        name="table_t",
    )(x)


def _final_ln_kernel(x1_ref, ft_ref, g2_ref, lg_ref, lb_ref, o_ref):
    o_ref[...] = _layer_norm(DN_ALPHA * x1_ref[...] + g2_ref[0] * ft_ref[...].T) * lg_ref[...] + lb_ref[...]


def _final_ln(x1, ffn_t, mod3, lg, lb, seq):
    t, d = x1.shape
    tm = TM_FINAL
    per_b = seq // tm
    row = lambda i: (i, 0)
    return pl.pallas_call(
        _final_ln_kernel,
        grid=(t // tm,),
        in_specs=[pl.BlockSpec((tm, d), row), pl.BlockSpec((d, tm), lambda i: (0, i)),
                  pl.BlockSpec((1, 1, d), lambda i: ((i // per_b) * 6 + 5, 0, 0)),
                  _resident(lg.shape), _resident(lb.shape)],
        out_specs=pl.BlockSpec((tm, d), row),
        out_shape=jax.ShapeDtypeStruct((t, d), F32),
        compiler_params=_params("arbitrary"),
        name="final_ln",
    )(x1, ffn_t, mod3, lg, lb)


def _rope_tables(seq, rot_dim):
    half = rot_dim // 2
    pos = jnp.arange(seq, dtype=F32)
    inv = ROPE_THETA ** (-jnp.arange(half, dtype=F32) * 2.0 / rot_dim)
    ang = pos[:, None] * inv[None, :]
    cos, sin = jnp.cos(ang), jnp.sin(ang)
    pad = V7X_LANES - rot_dim
    ones = jnp.ones((seq, pad), F32)
    zeros = jnp.zeros((seq, pad), F32)
    zh = jnp.zeros((seq, half), F32)
    return (jnp.concatenate([cos, cos, ones], 1),
            jnp.concatenate([-sin, zh, zeros], 1),
            jnp.concatenate([zh, sin, zeros], 1))


def _layer(x2, mod3, seq, w_in, g_q_lat, g_kv_lat, w_uq, w_uk, w_uv, g_out_a, g_out_b, w_o,
           ln1_g, ln1_b, w_pq, sub_key_1, sub_key_2, u_table, v_table, ln2_g, ln2_b):
    d = x2.shape[1]
    w_bf = w_in.astype(BF16)
    w_main = jnp.concatenate([w_bf[:, :2 * A_WIDTH], w_bf[:, 3 * A_WIDTH:]], axis=1)
    w_main = jnp.pad(w_main, ((0, 0), (0, (-w_main.shape[1]) % V7X_LANES)))
    wvt = w_bf[:, 2 * A_WIDTH:3 * A_WIDTH].T
    pad_q = MLA_QK_PAD - MLA_NOPE - MLA_ROPE
    wuq_r = jnp.pad(w_uq.astype(BF16).reshape(MLA_RANK, B_HEADS, MLA_NOPE + MLA_ROPE),
                    ((0, 0), (0, 0), (0, pad_q))).reshape(MLA_RANK, B_HEADS * MLA_QK_PAD)
    tabs_a = _rope_tables(seq, A_ROT_DIM)
    tabs_b = _rope_tables(seq, MLA_ROPE)
    qa, ka, vat, qb, kb, vbt = _in_proj(
        x2, mod3, w_main, wvt, wuq_r, w_uk.astype(BF16), w_uv.astype(BF16).T,
        g_q_lat.reshape(1, -1), g_kv_lat.reshape(1, -1), tabs_a, tabs_b, seq)

    bias_mix, bias_causal = _bias_tables(TQ_ATTN, seq)
    oa = _attention(qa, ka, vat, bias_mix, seq=seq, heads=A_HEADS, ek=HEAD_DIM, bias_all_keys=True)
    ob = _attention(qb, kb, vbt, bias_causal, seq=seq, heads=B_HEADS, ek=MLA_QK_PAD, bias_all_keys=False)

    sk = jnp.stack([sub_key_1, sub_key_2]).astype(BF16)
    x1, h2, st = _post_attn(oa, ob, x2, mod3, g_out_a.reshape(1, -1), g_out_b.reshape(1, -1),
                            w_o.astype(BF16), ln1_g.reshape(1, -1), ln1_b.reshape(1, -1),
                            w_pq.astype(BF16), sk, seq)
    e1, cnt, e2, r2 = _peer_topk(st)
    ffn_t = _peer_dense(h2, e1, cnt, e2, r2, _to_f8(u_table), _table_t(v_table))
    return _final_ln(x1, ffn_t, mod3, ln2_g.reshape(1, d), ln2_b.reshape(1, d), seq)


def kernel(x, c, w_ada, b_ada, w_in, g_q_lat, g_kv_lat, w_uq, w_uk, w_uv, g_out_a, g_out_b, w_o, ln1_g, ln1_b,
           w_pq, sub_key_1, sub_key_2, u_table, v_table, ln2_g, ln2_b):
    bsz, seq, d = x.shape
    x2 = x.reshape(bsz * seq, d)
    for l in range(w_ada.shape[0]):
        mod3 = _ada_mod(c, w_ada[l], b_ada[l]).reshape(bsz * 6, 1, d)
        x2 = _layer(x2, mod3, seq, w_in[l], g_q_lat[l], g_kv_lat[l], w_uq[l], w_uk[l], w_uv[l],
                    g_out_a[l], g_out_b[l], w_o[l], ln1_g[l], ln1_b[l], w_pq[l], sub_key_1[l], sub_key_2[l],
                    u_table[l], v_table[l], ln2_g[l], ln2_b[l])
    return x2.reshape(bsz, seq, d)
```

```python
import functools
import math

import jax
import jax.numpy as jnp
from jax import lax
from jax.experimental import pallas as pl
from jax.experimental.pallas import tpu as pltpu

F32 = jnp.float32
BF16 = jnp.bfloat16
F8 = jnp.float8_e4m3fn
F8_MAX = 448.0

D_MODEL = 2048
HEAD_DIM = 128
A_HEADS = 8
A_WIDTH = A_HEADS * HEAD_DIM
A_ROT_DIM = 32
ROPE_THETA = 500000.0
B_HEADS = 8
MLA_RANK = 512
MLA_NOPE = 128
MLA_ROPE = 64
MLA_V = 128
B_WIDTH = B_HEADS * MLA_V
MLA_QK_PAD = 256
PEER_HEADS = 8
PEER_N_KEYS = 128
PEER_TOPK = 16
DN_ALPHA = 2.0 ** 0.25

V7X_LANES = 128
V7X_SUBLANES = 8
V7X_BF16_SUBLANES = 16
V7X_VMEM_LIMIT = 56 * 1024 * 1024

TM_PROJ = 256
TQ_ATTN = 512
TM_PEER = 1024
TS_PEER = 512
TE_PEER = 1024
PIECE_PEER = 256
TM_FINAL = 512

NEG_BIG = -1e30
LOG2E = math.log2(math.e)

_NT = (((1,), (1,)), ((), ()))

_C_QA, _C_KA, _C_CQ, _C_CKV, _C_KR = 0, A_WIDTH, 2 * A_WIDTH, 2 * A_WIDTH + MLA_RANK, 2 * A_WIDTH + 2 * MLA_RANK


def _params(*sem):
    return pltpu.CompilerParams(dimension_semantics=sem, vmem_limit_bytes=V7X_VMEM_LIMIT)


def _resident(shape):
    nd = len(shape)
    return pl.BlockSpec(shape, lambda *_: (0,) * nd, pipeline_mode=pl.Buffered(1))


def _layer_norm(x, eps=1e-5):
    mu = jnp.mean(x, axis=-1, keepdims=True)
    xc = x - mu
    var = jnp.mean(xc * xc, axis=-1, keepdims=True)
    return xc * lax.rsqrt(var + eps)


def _rms_norm(x, g, eps=1e-6):
    return x * lax.rsqrt(jnp.mean(x * x, axis=-1, keepdims=True) + eps) * g


def _to_f8(x):
    big = jnp.asarray(F8_MAX, x.dtype)
    return lax.clamp(-big, x, big).astype(F8)


def _rope_lanes(x, cos, sin_lo, sin_hi, half):
    return (x * cos + pltpu.roll(x, V7X_LANES - half, 1) * sin_lo + pltpu.roll(x, half, 1) * sin_hi)


def _ada_kernel(c_ref, w_ref, b_ref, o_ref):
    c = c_ref[...]
    act = (c / (1.0 + jnp.exp(-c))).astype(BF16)
    o_ref[...] = jnp.dot(act, w_ref[...].astype(BF16), preferred_element_type=F32) + b_ref[...]


def _ada_mod(c, w_ada, b_ada):
    bsz, d = c.shape
    n = w_ada.shape[1]
    tn = 1024
    return pl.pallas_call(
        _ada_kernel,
        grid=(n // tn,),
        in_specs=[pl.BlockSpec((bsz, d), lambda j: (0, 0)),
                  pl.BlockSpec((d, tn), lambda j: (0, j)),
                  pl.BlockSpec((1, tn), lambda j: (0, j))],
        out_specs=pl.BlockSpec((bsz, tn), lambda j: (0, j)),
        out_shape=jax.ShapeDtypeStruct((bsz, n), F32),
        compiler_params=_params("arbitrary"),
        name="ada_mod",
    )(c, w_ada, b_ada.reshape(1, n))


def _in_proj_kernel(x_ref, sc_ref, sh_ref, w_ref, wvt_ref, wuq_ref, wuk_ref, wuvt_ref, gq_ref, gkv_ref,
                    ca_ref, sal_ref, sah_ref, cb_ref, sbl_ref, sbh_ref,
                    qa_ref, ka_ref, vat_ref, qb_ref, kb_ref, vbt_ref, *, scale_a, scale_b):
    h = (_layer_norm(x_ref[...]) * (1.0 + sc_ref[0]) + sh_ref[0]).astype(BF16)

    def proj(c0, width):
        return jnp.dot(h, w_ref[:, c0:c0 + width], preferred_element_type=F32)

    ca, sal, sah = ca_ref[...], sal_ref[...], sah_ref[...]
    cb, sbl, sbh = cb_ref[...], sbl_ref[...], sbh_ref[...]
    chunk = 512
    for c_base, out_ref, scale in ((_C_QA, qa_ref, scale_a), (_C_KA, ka_ref, None)):
        for cc in range(A_WIDTH // chunk):
            acc = proj(c_base + cc * chunk, chunk)
            for hh in range(chunk // HEAD_DIM):
                r = _rope_lanes(acc[:, hh * HEAD_DIM:(hh + 1) * HEAD_DIM], ca, sal, sah, A_ROT_DIM // 2)
                if scale is not None:
                    r = r * scale
                c0 = cc * chunk + hh * HEAD_DIM
                out_ref[:, c0:c0 + HEAD_DIM] = r.astype(BF16)
    vat_ref[...] = lax.dot_general(wvt_ref[...], h, _NT, preferred_element_type=F32).astype(BF16)
    cq = _rms_norm(proj(_C_CQ, MLA_RANK), gq_ref[...]).astype(BF16)
    ckv = _rms_norm(proj(_C_CKV, MLA_RANK), gkv_ref[...]).astype(BF16)
    kr = _rope_lanes(proj(_C_KR, V7X_LANES), cb, sbl, sbh, MLA_ROPE // 2).astype(BF16)
    for cc in range(B_HEADS * MLA_QK_PAD // chunk):
        qq = jnp.dot(cq, wuq_ref[:, cc * chunk:(cc + 1) * chunk], preferred_element_type=F32)
        for hh in range(chunk // MLA_QK_PAD):
            c0 = cc * chunk + hh * MLA_QK_PAD
            nope = qq[:, hh * MLA_QK_PAD:hh * MLA_QK_PAD + MLA_NOPE]
            rp = qq[:, hh * MLA_QK_PAD + MLA_NOPE:(hh + 1) * MLA_QK_PAD]
            qb_ref[:, c0:c0 + MLA_NOPE] = (nope * scale_b).astype(BF16)
            qb_ref[:, c0 + MLA_NOPE:c0 + MLA_QK_PAD] = (
                _rope_lanes(rp, cb, sbl, sbh, MLA_ROPE // 2) * scale_b).astype(BF16)
    for cc in range(B_WIDTH // chunk):
        kn = jnp.dot(ckv, wuk_ref[:, cc * chunk:(cc + 1) * chunk], preferred_element_type=F32)
        for hh in range(chunk // MLA_NOPE):
            head = cc * (chunk // MLA_NOPE) + hh
            kb_ref[:, head * MLA_QK_PAD:head * MLA_QK_PAD + MLA_NOPE] = (
                kn[:, hh * MLA_NOPE:(hh + 1) * MLA_NOPE].astype(BF16))
            kb_ref[:, head * MLA_QK_PAD + MLA_NOPE:(head + 1) * MLA_QK_PAD] = kr
    vbt_ref[...] = lax.dot_general(wuvt_ref[...], ckv, _NT, preferred_element_type=F32).astype(BF16)


def _in_proj(x2, mod3, w_main, wvt, wuq_r, wuk, wuvt, gq, gkv, tabs_a, tabs_b, seq):
    t, d = x2.shape
    tm = TM_PROJ
    per_b = seq // tm
    row = lambda i: (i, 0)
    col = lambda i: (0, i)
    tab = lambda i: (i % per_b, 0)
    mod_spec = lambda k: pl.BlockSpec((1, 1, d), lambda i: ((i // per_b) * 6 + k, 0, 0))
    qk_b = B_HEADS * MLA_QK_PAD
    out_specs = [pl.BlockSpec((tm, A_WIDTH), row), pl.BlockSpec((tm, A_WIDTH), row),
                 pl.BlockSpec((A_WIDTH, tm), col),
                 pl.BlockSpec((tm, qk_b), row), pl.BlockSpec((tm, qk_b), row),
                 pl.BlockSpec((B_WIDTH, tm), col)]
    out_shape = [jax.ShapeDtypeStruct(s, BF16) for s in
                 ((t, A_WIDTH), (t, A_WIDTH), (A_WIDTH, t), (t, qk_b), (t, qk_b), (B_WIDTH, t))]
    return pl.pallas_call(
        functools.partial(_in_proj_kernel, scale_a=HEAD_DIM ** -0.5 * LOG2E,
                          scale_b=(MLA_NOPE + MLA_ROPE) ** -0.5 * LOG2E),
        grid=(t // tm,),
        in_specs=[pl.BlockSpec((tm, d), row), mod_spec(1), mod_spec(0),
                  _resident(w_main.shape), _resident(wvt.shape), _resident(wuq_r.shape), _resident(wuk.shape),
                  _resident(wuvt.shape), _resident(gq.shape), _resident(gkv.shape)]
                 + [pl.BlockSpec((tm, V7X_LANES), tab)] * 6,
        out_specs=out_specs,
        out_shape=out_shape,
        compiler_params=_params("arbitrary"),
        name="in_proj",
    )(x2, mod3, mod3, w_main, wvt, wuq_r, wuk, wuvt, gq, gkv, *tabs_a, *tabs_b)


def _attn_kernel(q_ref, k_ref, vt_ref, bias_ref, o_ref, *, tq, bias_all_keys):
    seq = q_ref.shape[0]
    nq = seq // tq
    nb = bias_ref.shape[0]

    def key_parts(qi):
        n = (qi + 1) * tq
        if bias_all_keys:
            return [(0, n, True)]
        return ([(0, n - tq, False)] if qi > 0 else []) + [(n - tq, n, True)]

    def score_tiles(qi):
        q = q_ref[qi * tq:(qi + 1) * tq, :]
        tiles = []
        for k0, k1, biased in key_parts(qi):
            st = lax.dot_general(k_ref[k0:k1, :], q, _NT, preferred_element_type=F32)
            tiles.append(st + bias_ref[nb - (k1 - k0):nb, :] if biased else st)
        return tiles

    ahead = score_tiles(0)
    for qi in range(nq):
        parts, scores = key_parts(qi), ahead
        if qi + 1 < nq:
            ahead = score_tiles(qi + 1)
        m = jnp.max(scores[0], axis=0, keepdims=True)
        for st in scores[1:]:
            m = jnp.maximum(m, jnp.max(st, axis=0, keepdims=True))
        l = jnp.zeros_like(m)
        o_t = jnp.zeros((vt_ref.shape[0], tq), F32)
        for (k0, k1, _), st in zip(parts, scores):
            p = jnp.exp2(st - m)
            l = l + jnp.sum(p, axis=0, keepdims=True)
            o_t = o_t + jnp.dot(vt_ref[:, k0:k1], p.astype(BF16), preferred_element_type=F32)
        o_ref[qi * tq:(qi + 1) * tq, :] = (o_t * (1.0 / l)).T


def _attention(q, k, vt, bias, *, seq, heads, ek, bias_all_keys):
    t = q.shape[0]
    bsz = t // seq
    ev = vt.shape[0] // heads
    return pl.pallas_call(
        functools.partial(_attn_kernel, tq=TQ_ATTN, bias_all_keys=bias_all_keys),
        grid=(bsz, heads),
        in_specs=[pl.BlockSpec((seq, ek), lambda b, h: (b, h)),
                  pl.BlockSpec((seq, ek), lambda b, h: (b, h)),
                  pl.BlockSpec((ev, seq), lambda b, h: (h, b)),
                  _resident(bias.shape)],
        out_specs=pl.BlockSpec((seq, ev), lambda b, h: (b, h)),
        out_shape=jax.ShapeDtypeStruct((t, heads * ev), F32),
        compiler_params=_params("arbitrary", "arbitrary"),
        name="attention",
    )(q, k, vt, bias)


def _bias_tables(tq, seq):
    d = ((seq - tq) + jnp.arange(tq, dtype=jnp.int32)[None, :] - jnp.arange(seq, dtype=jnp.int32)[:, None])
    ok = d >= 0
    mult = ((d <= 128).astype(F32) + ((d % 4 == 0) & (d <= 512)).astype(F32)
            + ((d % 16 == 0) & (d <= 2048)).astype(F32))
    mix = jnp.where(ok & (mult > 0), jnp.log2(jnp.maximum(mult, 1.0)), NEG_BIG).astype(F32)
    causal = jnp.where(ok[seq - tq:], 0.0, NEG_BIG).astype(F32)
    return mix, causal


def _post_attn_kernel(oa_ref, ob_ref, x_ref, g1_ref, sc_ref, sh_ref, ga_ref, gb_ref, wo_ref, l1g_ref, l1b_ref,
                      wpq_ref, sk_ref, x1_ref, h2_ref, e1_ref, cnt_ref, e2_ref, r2_ref, st_sc, na_sc, nb_sc, hq_sc):
    step = pl.program_id(0)

    @pl.when(step == 0)
    def _():
        st_sc[...] = jnp.zeros(st_sc.shape, F32)

    cur = step % 2
    prev = 1 - cur

    def select_head(hh):
        return _topk_head(lambda r0: st_sc[prev, r0:r0 + V7X_SUBLANES, :], hh, e1_ref, cnt_ref, e2_ref, r2_ref)

    chunk = 512
    n_chunk = D_MODEL // chunk
    assert 2 * n_chunk == PEER_HEADS and chunk == 4 * PEER_N_KEYS

    na_sc[...] = _rms_norm(oa_ref[...], ga_ref[...]).astype(BF16)
    nb_sc[...] = _rms_norm(ob_ref[...], gb_ref[...]).astype(BF16)
    mix_chunks = []
    for cc in range(n_chunk):
        cols = slice(cc * chunk, (cc + 1) * chunk)
        mix_chunks.append(jnp.dot(na_sc[...], wo_ref[0:A_WIDTH, cols], preferred_element_type=F32)
                          + jnp.dot(nb_sc[...], wo_ref[A_WIDTH:A_WIDTH + B_WIDTH, cols],
                                    preferred_element_type=F32))
        _order_after(na_sc, select_head(cc))
    mix = jnp.concatenate(mix_chunks, axis=1)
    x1 = _layer_norm(DN_ALPHA * x_ref[...] + g1_ref[0] * mix) * l1g_ref[...] + l1b_ref[...]
    x1_ref[...] = x1
    h2_f32 = _layer_norm(x1) * (1.0 + sc_ref[0]) + sh_ref[0]
    h2_ref[...] = _to_f8(h2_f32)
    hq_sc[...] = h2_f32.astype(BF16)
    for cc in range(n_chunk):
        q = jnp.dot(hq_sc[...], wpq_ref[:, cc * chunk:(cc + 1) * chunk], preferred_element_type=F32).astype(BF16)
        for j in range(chunk // PEER_N_KEYS):
            r0 = cc * chunk + j * PEER_N_KEYS
            st_sc[cur, r0:r0 + PEER_N_KEYS, :] = lax.dot_general(
                sk_ref[j % 2], q[:, j * PEER_N_KEYS:(j + 1) * PEER_N_KEYS], _NT, preferred_element_type=F32)
        _order_after(hq_sc, select_head(n_chunk + cc))


def _post_attn(oa, ob, x2, mod3, ga, gb, wo, l1g, l1b, wpq, sk, seq):
    t, d = x2.shape
    tm = TM_PROJ
    per_b = seq // tm
    n_tile = t // tm
    tile = lambda i: jnp.minimum(i, n_tile - 1)
    done = lambda i: jnp.maximum(i - 1, 0)
    row = lambda i: (tile(i), 0)
    sel = lambda i: (0, done(i))
    mod_spec = lambda k: pl.BlockSpec((1, 1, d), lambda i: ((tile(i) // per_b) * 6 + k, 0, 0))
    n = PEER_HEADS * PEER_N_KEYS
    return pl.pallas_call(
        _post_attn_kernel,
        grid=(n_tile + 1,),
        in_specs=[pl.BlockSpec((tm, A_WIDTH), row), pl.BlockSpec((tm, B_WIDTH), row), pl.BlockSpec((tm, d), row),
                  mod_spec(2), mod_spec(4), mod_spec(3),
                  _resident(ga.shape), _resident(gb.shape), _resident(wo.shape),
                  _resident(l1g.shape), _resident(l1b.shape), _resident(wpq.shape), _resident(sk.shape)],
        out_specs=[pl.BlockSpec((tm, d), row), pl.BlockSpec((tm, d), row)] + [pl.BlockSpec((n, tm), sel)] * 4,
        out_shape=[jax.ShapeDtypeStruct((t, d), F32), jax.ShapeDtypeStruct((t, d), F8),
                   jax.ShapeDtypeStruct((n, t), F32), jax.ShapeDtypeStruct((n, t), F32),
                   jax.ShapeDtypeStruct((n, t), BF16), jax.ShapeDtypeStruct((n, t), BF16)],
        scratch_shapes=[pltpu.VMEM((2, 2 * n, tm), F32), pltpu.VMEM((tm, A_WIDTH), BF16),
                        pltpu.VMEM((tm, B_WIDTH), BF16), pltpu.VMEM((tm, d), BF16)],
        compiler_params=_params("arbitrary"),
        name="post_attn",
    )(oa, ob, x2, mod3, mod3, mod3, ga, gb, wo, l1g, l1b, wpq, sk)


_GROUPS = PEER_N_KEYS // V7X_SUBLANES


def _oddeven_merge_sort_pairs(n):
    pairs = []
    p = 1
    while p < n:
        k = p
        while k >= 1:
            for j in range(k % p, n - k, 2 * k):
                for i in range(min(k, n - j - k)):
                    if (i + j) // (2 * p) == (i + j + k) // (2 * p):
                        pairs.append((i + j, i + j + k))
            k //= 2
        p *= 2
    return pairs


_SORT16 = _oddeven_merge_sort_pairs(_GROUPS)


def _sublane_max(x):
    for shift in (4, 2, 1):
        x = jnp.maximum(x, pltpu.roll(x, shift, 0))
    return x


def _sorted_top16(groups):
    c = list(groups)
    for i, j in _SORT16:
        c[i], c[j] = jnp.maximum(c[i], c[j]), jnp.minimum(c[i], c[j])
    out = []
    for r in range(PEER_TOPK):
        m = _sublane_max(c[0])
        out.append(m)
        if r + 1 < PEER_TOPK:
            hit = c[0] == m
            c = [jnp.where(hit, c[k + 1], c[k]) for k in range(len(c) - 1)]
    return out


def _count_leading(pred, v):
    p8 = pred(v[7])
    p4 = pred(jnp.where(p8, v[11], v[3]))
    p2 = pred(jnp.where(p8, jnp.where(p4, v[13], v[9]), jnp.where(p4, v[5], v[1])))
    hi = jnp.where(p4, jnp.where(p2, v[14], v[12]), jnp.where(p2, v[10], v[8]))
    lo = jnp.where(p4, jnp.where(p2, v[6], v[4]), jnp.where(p2, v[2], v[0]))
    p1 = pred(jnp.where(p8, hi, lo))
    p16 = pred(v[15])
    one, zero = jnp.float32(1.0), jnp.float32(0.0)
    return (jnp.where(p8, 8.0, zero) + jnp.where(p4, 4.0, zero) + jnp.where(p2, 2.0, zero)
            + jnp.where(p1, one, zero) + jnp.where(p16, one, zero))


def _topk_head(score_rows, hh, e1_ref, cnt_ref, e2_ref, r2_ref):
    r0 = hh * 2 * PEER_N_KEYS
    g1 = [score_rows(r0 + V7X_SUBLANES * k) for k in range(_GROUPS)]
    g2 = [score_rows(r0 + PEER_N_KEYS + V7X_SUBLANES * k) for k in range(_GROUPS)]
    sub = lax.broadcasted_iota(jnp.int32, g1[0].shape, 0)

    def by_sublane(rows):
        out = rows[0]
        for j in range(1, V7X_SUBLANES):
            out = jnp.where(sub == j, rows[j], out)
        return out

    v1 = _sorted_top16(g1)
    v2 = _sorted_top16(g2)
    v2_lo, v2_hi, v1_hi = by_sublane(v2[:8]), by_sublane(v2[8:]), by_sublane(v1[8:])
    cand = [v1[0] + v2_lo, v1[0] + v2_hi, v1[1] + v2_lo]
    for a in range(2, 8):
        cand.append(jnp.where(sub < PEER_TOPK // (a + 1), v1[a] + v2_lo, -jnp.inf))
    cand.append(v1_hi + v2[0])
    cmax = v1[0] + v2[0]
    z = jnp.zeros_like(cmax)
    tau = cmax
    for r in range(PEER_TOPK):
        tau = cand[0]
        for blk in cand[1:]:
            tau = jnp.maximum(tau, blk)
        tau = _sublane_max(tau)
        z = z + jnp.exp(tau - cmax)
        if r + 1 < PEER_TOPK:
            cand = [jnp.where(blk == tau, -jnp.inf, blk) for blk in cand]
    inv_z = 1.0 / z
    for k in range(_GROUPS):
        rows = slice(hh * PEER_N_KEYS + V7X_SUBLANES * k, hh * PEER_N_KEYS + V7X_SUBLANES * (k + 1))
        x1 = g1[k]
        e1_ref[rows, :] = jnp.exp(x1 - v1[0]) * inv_z
        cnt_ref[rows, :] = _count_leading(lambda vb: x1 + vb >= tau, v2)
    for k in range(0, _GROUPS, 2):
        rows = slice(hh * PEER_N_KEYS + V7X_SUBLANES * k, hh * PEER_N_KEYS + V7X_SUBLANES * (k + 2))
        e2_ref[rows, :] = jnp.concatenate(
            [jnp.exp(g2[k + j] - v2[0]) for j in range(2)], axis=0).astype(BF16)
        r2_ref[rows, :] = jnp.concatenate(
            [_count_leading(lambda vb, x=g2[k + j]: vb > x, v2) for j in range(2)], axis=0).astype(BF16)
    return inv_z


def _order_after(ref, v):
    bits = pltpu.bitcast(v[0:V7X_SUBLANES, 0:V7X_LANES], jnp.uint32)
    zero = pltpu.bitcast(lax.shift_right_logical(bits, jnp.uint32(32)), F32)
    tile = (slice(0, V7X_BF16_SUBLANES), slice(0, V7X_LANES))
    ref[tile] = ref[tile] + jnp.concatenate([zero, zero], axis=0).astype(ref.dtype)


def _gelu(x):
    half = 0.5 * x
    return half + half * lax.erf(x * (2.0 ** -0.5))


def _row_bf16(ref, row, cols, rows_out):
    r = jnp.broadcast_to(ref[pl.ds(row, 1), cols], (V7X_BF16_SUBLANES, cols.stop - cols.start)).astype(BF16)
    return jnp.concatenate([r] * (rows_out // V7X_BF16_SUBLANES), axis=0)


def _peer_dense_kernel(h2_ref, e1_ref, cnt_ref, e2_ref, r2_ref, u_ref, vt_ref, o_ref, a_sc, w_sc, g_sc, *, te, ts):
    e = pl.program_id(1)
    n_i1 = te // PEER_N_KEYS
    tm = h2_ref.shape[0]

    @pl.when(e == 0)
    def _():
        o_ref[...] = jnp.zeros(o_ref.shape, F32)

    chains = [slice(s * ts, (s + 1) * ts) for s in range(tm // ts)]
    d_model = vt_ref.shape[0]
    n_piece = te // PIECE_PEER
    per_piece = PIECE_PEER // PEER_N_KEYS
    d_piece = d_model // n_piece

    def first_piece(c, k):
        rows = slice(k * PIECE_PEER, (k + 1) * PIECE_PEER)
        a_sc[c, rows, :] = lax.dot_general(u_ref[rows, :], h2_ref[chains[c], :], _NT,
                                           preferred_element_type=F32)

    def gate_tile(c, ii):
        i1 = e * n_i1 + ii
        w = jnp.zeros((PEER_N_KEYS, ts), BF16)
        for hh in range(PEER_HEADS):
            rows = slice(hh * PEER_N_KEYS, (hh + 1) * PEER_N_KEYS)
            e1 = _row_bf16(e1_ref, hh * PEER_N_KEYS + i1, chains[c], PEER_N_KEYS)
            cnt = _row_bf16(cnt_ref, hh * PEER_N_KEYS + i1, chains[c], PEER_N_KEYS)
            w = w + jnp.where(r2_ref[rows, chains[c]] < cnt, e2_ref[rows, chains[c]], jnp.zeros_like(w)) * e1
        w_sc[c, ii * PEER_N_KEYS:(ii + 1) * PEER_N_KEYS, :] = w

    def act_tile(c, ii):
        rows = slice(ii * PEER_N_KEYS, (ii + 1) * PEER_N_KEYS)
        g_sc[c, rows, :] = _to_f8(_gelu(a_sc[c, rows, :].astype(BF16)) * w_sc[c, rows, :])

    def second_piece(c, k):
        rows = slice(k * d_piece, (k + 1) * d_piece)
        o_ref[rows, chains[c]] += jnp.dot(vt_ref[rows, :], g_sc[c], preferred_element_type=F32)

    def tiles_of(k):
        return range(k * per_piece, (k + 1) * per_piece)

    for c in range(len(chains)):
        for k in range(n_piece):
            first_piece(c, k)
            for ii in tiles_of(k):
                if c == 0:
                    gate_tile(0, ii)
                if c == 1 or len(chains) == 1:
                    act_tile(0, ii)
    for c in range(len(chains)):
        for k in range(n_piece):
            second_piece(c, k)
            if c + 1 < len(chains):
                for ii in tiles_of(k):
                    gate_tile(c + 1, ii)
        if c + 1 < len(chains):
            for ii in range(n_i1):
                act_tile(c + 1, ii)


def _peer_dense(h2, e1, cnt, e2, r2, u_bf, vt_bf):
    t, d = h2.shape
    n_e = u_bf.shape[0]
    n = e1.shape[0]
    tm, te = TM_PEER, TE_PEER
    n_chain = tm // TS_PEER
    tok = lambda i, e: (0, i)
    once = pl.Buffered(1)
    return pl.pallas_call(
        functools.partial(_peer_dense_kernel, te=te, ts=TS_PEER),
        grid=(t // tm, n_e // te),
        in_specs=[pl.BlockSpec((tm, d), lambda i, e: (i, 0)),
                  pl.BlockSpec((n, tm), tok), pl.BlockSpec((n, tm), tok),
                  pl.BlockSpec((n, tm), tok), pl.BlockSpec((n, tm), tok),
                  pl.BlockSpec((te, d), lambda i, e: (e, 0)),
                  pl.BlockSpec((d, te), lambda i, e: (0, e))],
        out_specs=pl.BlockSpec((d, tm), tok, pipeline_mode=once),
        out_shape=jax.ShapeDtypeStruct((d, t), F32),
        scratch_shapes=[pltpu.VMEM((n_chain, te, TS_PEER), F32), pltpu.VMEM((n_chain, te, TS_PEER), BF16),
                        pltpu.VMEM((n_chain, te, TS_PEER), F8)],
        compiler_params=_params("arbitrary", "arbitrary"),
        name="peer_dense",
    )(h2, e1, cnt, e2, r2, u_bf, vt_bf)


def _table_t_kernel(x_ref, o_ref):
    o_ref[...] = _to_f8(x_ref[...].T)


def _table_t(x):
    rows, cols = x.shape
    tr = 512
    return pl.pallas_call(
        _table_t_kernel,
        grid=(rows // tr,),
        in_specs=[pl.BlockSpec((tr, cols), lambda i: (i, 0))],
        out_specs=pl.BlockSpec((cols, tr), lambda i: (0, i)),
        out_shape=jax.ShapeDtypeStruct((cols, rows), F8),
        compiler_params=_params("arbitrary"),
        name="table_t",
    )(x)


def _final_ln_kernel(x1_ref, ft_ref, g2_ref, lg_ref, lb_ref, o_ref):
    o_ref[...] = _layer_norm(DN_ALPHA * x1_ref[...] + g2_ref[0] * ft_ref[...].T) * lg_ref[...] + lb_ref[...]


def _final_ln(x1, ffn_t, mod3, lg, lb, seq):
    t, d = x1.shape
    tm = TM_FINAL
    per_b = seq // tm
    row = lambda i: (i, 0)
    return pl.pallas_call(
        _final_ln_kernel,
        grid=(t // tm,),
        in_specs=[pl.BlockSpec((tm, d), row), pl.BlockSpec((d, tm), lambda i: (0, i)),
                  pl.BlockSpec((1, 1, d), lambda i: ((i // per_b) * 6 + 5, 0, 0)),
                  _resident(lg.shape), _resident(lb.shape)],
        out_specs=pl.BlockSpec((tm, d), row),
        out_shape=jax.ShapeDtypeStruct((t, d), F32),
        compiler_params=_params("arbitrary"),
        name="final_ln",
    )(x1, ffn_t, mod3, lg, lb)


def _rope_tables(seq, rot_dim):
    half = rot_dim // 2
    pos = jnp.arange(seq, dtype=F32)
    inv = ROPE_THETA ** (-jnp.arange(half, dtype=F32) * 2.0 / rot_dim)
    ang = pos[:, None] * inv[None, :]
    cos, sin = jnp.cos(ang), jnp.sin(ang)
    pad = V7X_LANES - rot_dim
    ones = jnp.ones((seq, pad), F32)
    zeros = jnp.zeros((seq, pad), F32)
    zh = jnp.zeros((seq, half), F32)
    return (jnp.concatenate([cos, cos, ones], 1),
            jnp.concatenate([-sin, zh, zeros], 1),
            jnp.concatenate([zh, sin, zeros], 1))


def _layer(x2, mod3, seq, w_in, g_q_lat, g_kv_lat, w_uq, w_uk, w_uv, g_out_a, g_out_b, w_o,
           ln1_g, ln1_b, w_pq, sub_key_1, sub_key_2, u_table, v_table, ln2_g, ln2_b):
    d = x2.shape[1]
    w_bf = w_in.astype(BF16)
    w_main = jnp.concatenate([w_bf[:, :2 * A_WIDTH], w_bf[:, 3 * A_WIDTH:]], axis=1)
    w_main = jnp.pad(w_main, ((0, 0), (0, (-w_main.shape[1]) % V7X_LANES)))
    wvt = w_bf[:, 2 * A_WIDTH:3 * A_WIDTH].T
    pad_q = MLA_QK_PAD - MLA_NOPE - MLA_ROPE
    wuq_r = jnp.pad(w_uq.astype(BF16).reshape(MLA_RANK, B_HEADS, MLA_NOPE + MLA_ROPE),
                    ((0, 0), (0, 0), (0, pad_q))).reshape(MLA_RANK, B_HEADS * MLA_QK_PAD)
    tabs_a = _rope_tables(seq, A_ROT_DIM)
    tabs_b = _rope_tables(seq, MLA_ROPE)
    qa, ka, vat, qb, kb, vbt = _in_proj(
        x2, mod3, w_main, wvt, wuq_r, w_uk.astype(BF16), w_uv.astype(BF16).T,
        g_q_lat.reshape(1, -1), g_kv_lat.reshape(1, -1), tabs_a, tabs_b, seq)

    bias_mix, bias_causal = _bias_tables(TQ_ATTN, seq)
    oa = _attention(qa, ka, vat, bias_mix, seq=seq, heads=A_HEADS, ek=HEAD_DIM, bias_all_keys=True)
    ob = _attention(qb, kb, vbt, bias_causal, seq=seq, heads=B_HEADS, ek=MLA_QK_PAD, bias_all_keys=False)

    sk = jnp.stack([sub_key_1, sub_key_2]).astype(BF16)
    x1, h2, e1, cnt, e2, r2 = _post_attn(oa, ob, x2, mod3, g_out_a.reshape(1, -1), g_out_b.reshape(1, -1),
                                         w_o.astype(BF16), ln1_g.reshape(1, -1), ln1_b.reshape(1, -1),
                                         w_pq.astype(BF16), sk, seq)
    ffn_t = _peer_dense(h2, e1, cnt, e2, r2, _to_f8(u_table), _table_t(v_table))
    return _final_ln(x1, ffn_t, mod3, ln2_g.reshape(1, d), ln2_b.reshape(1, d), seq)


def kernel(x, c, w_ada, b_ada, w_in, g_q_lat, g_kv_lat, w_uq, w_uk, w_uv, g_out_a, g_out_b, w_o, ln1_g, ln1_b,
           w_pq, sub_key_1, sub_key_2, u_table, v_table, ln2_g, ln2_b):
    bsz, seq, d = x.shape
    x2 = x.reshape(bsz * seq, d)
    for l in range(w_ada.shape[0]):
        mod3 = _ada_mod(c, w_ada[l], b_ada[l]).reshape(bsz * 6, 1, d)
        x2 = _layer(x2, mod3, seq, w_in[l], g_q_lat[l], g_kv_lat[l], w_uq[l], w_uk[l], w_uv[l],
                    g_out_a[l], g_out_b[l], w_o[l], ln1_g[l], ln1_b[l], w_pq[l], sub_key_1[l], sub_key_2[l],
                    u_table[l], v_table[l], ln2_g[l], ln2_b[l])
    return x2.reshape(bsz, seq, d)
```

```python
import functools
import math

import jax
import jax.numpy as jnp
from jax import lax
from jax.experimental import pallas as pl
from jax.experimental.pallas import tpu as pltpu

F32 = jnp.float32
BF16 = jnp.bfloat16
F8 = jnp.float8_e4m3fn
F8_MAX = 448.0

D_MODEL = 2048
HEAD_DIM = 128
A_HEADS = 8
A_WIDTH = A_HEADS * HEAD_DIM
A_ROT_DIM = 32
ROPE_THETA = 500000.0
B_HEADS = 8
MLA_RANK = 512
MLA_NOPE = 128
MLA_ROPE = 64
MLA_V = 128
B_WIDTH = B_HEADS * MLA_V
MLA_QK_PAD = 256
PEER_HEADS = 8
PEER_N_KEYS = 128
PEER_TOPK = 16
DN_ALPHA = 2.0 ** 0.25

V7X_LANES = 128
V7X_SUBLANES = 8
V7X_BF16_SUBLANES = 16
V7X_VMEM_LIMIT = 56 * 1024 * 1024

TM_PROJ = 256
TQ_ATTN = 512
TL_TOPK = 512
TM_PEER = 1024
TS_PEER = 512
TE_PEER = 1024
PIECE_PEER = 256
TM_FINAL = 512

NEG_BIG = -1e30
LOG2E = math.log2(math.e)

_NT = (((1,), (1,)), ((), ()))

_C_QA, _C_KA, _C_CQ, _C_CKV, _C_KR = 0, A_WIDTH, 2 * A_WIDTH, 2 * A_WIDTH + MLA_RANK, 2 * A_WIDTH + 2 * MLA_RANK


def _params(*sem):
    return pltpu.CompilerParams(dimension_semantics=sem, vmem_limit_bytes=V7X_VMEM_LIMIT)


def _resident(shape):
    nd = len(shape)
    return pl.BlockSpec(shape, lambda *_: (0,) * nd, pipeline_mode=pl.Buffered(1))


def _layer_norm(x, eps=1e-5):
    mu = jnp.mean(x, axis=-1, keepdims=True)
    xc = x - mu
    var = jnp.mean(xc * xc, axis=-1, keepdims=True)
    return xc * lax.rsqrt(var + eps)


def _rms_norm(x, g, eps=1e-6):
    return x * lax.rsqrt(jnp.mean(x * x, axis=-1, keepdims=True) + eps) * g


def _to_f8(x):
    big = jnp.asarray(F8_MAX, x.dtype)
    return lax.clamp(-big, x, big).astype(F8)


def _rope_lanes(x, cos, sin_lo, sin_hi, half):
    return (x * cos + pltpu.roll(x, V7X_LANES - half, 1) * sin_lo + pltpu.roll(x, half, 1) * sin_hi)


def _ada_kernel(c_ref, w_ref, b_ref, o_ref):
    c = c_ref[...]
    act = (c / (1.0 + jnp.exp(-c))).astype(BF16)
    o_ref[...] = jnp.dot(act, w_ref[...].astype(BF16), preferred_element_type=F32) + b_ref[...]


def _ada_mod(c, w_ada, b_ada):
    bsz, d = c.shape
    n = w_ada.shape[1]
    tn = 1024
    return pl.pallas_call(
        _ada_kernel,
        grid=(n // tn,),
        in_specs=[pl.BlockSpec((bsz, d), lambda j: (0, 0)),
                  pl.BlockSpec((d, tn), lambda j: (0, j)),
                  pl.BlockSpec((1, tn), lambda j: (0, j))],
        out_specs=pl.BlockSpec((bsz, tn), lambda j: (0, j)),
        out_shape=jax.ShapeDtypeStruct((bsz, n), F32),
        compiler_params=_params("arbitrary"),
        name="ada_mod",
    )(c, w_ada, b_ada.reshape(1, n))


def _in_proj_kernel(x_ref, sc_ref, sh_ref, w_ref, wvt_ref, wuq_ref, wuk_ref, wuvt_ref, gq_ref, gkv_ref,
                    ca_ref, sal_ref, sah_ref, cb_ref, sbl_ref, sbh_ref,
                    qa_ref, ka_ref, vat_ref, qb_ref, kb_ref, vbt_ref, *, scale_a, scale_b):
    h = (_layer_norm(x_ref[...]) * (1.0 + sc_ref[0]) + sh_ref[0]).astype(BF16)

    def proj(c0, width):
        return jnp.dot(h, w_ref[:, c0:c0 + width], preferred_element_type=F32)

    ca, sal, sah = ca_ref[...], sal_ref[...], sah_ref[...]
    cb, sbl, sbh = cb_ref[...], sbl_ref[...], sbh_ref[...]
    chunk = 512
    for c_base, out_ref, scale in ((_C_QA, qa_ref, scale_a), (_C_KA, ka_ref, None)):
        for cc in range(A_WIDTH // chunk):
            acc = proj(c_base + cc * chunk, chunk)
            for hh in range(chunk // HEAD_DIM):
                r = _rope_lanes(acc[:, hh * HEAD_DIM:(hh + 1) * HEAD_DIM], ca, sal, sah, A_ROT_DIM // 2)
                if scale is not None:
                    r = r * scale
                c0 = cc * chunk + hh * HEAD_DIM
                out_ref[:, c0:c0 + HEAD_DIM] = r.astype(BF16)
    vat_ref[...] = lax.dot_general(wvt_ref[...], h, _NT, preferred_element_type=F32).astype(BF16)
    cq = _rms_norm(proj(_C_CQ, MLA_RANK), gq_ref[...]).astype(BF16)
    ckv = _rms_norm(proj(_C_CKV, MLA_RANK), gkv_ref[...]).astype(BF16)
    kr = _rope_lanes(proj(_C_KR, V7X_LANES), cb, sbl, sbh, MLA_ROPE // 2).astype(BF16)
    for cc in range(B_HEADS * MLA_QK_PAD // chunk):
        qq = jnp.dot(cq, wuq_ref[:, cc * chunk:(cc + 1) * chunk], preferred_element_type=F32)
        for hh in range(chunk // MLA_QK_PAD):
            c0 = cc * chunk + hh * MLA_QK_PAD
            nope = qq[:, hh * MLA_QK_PAD:hh * MLA_QK_PAD + MLA_NOPE]
            rp = qq[:, hh * MLA_QK_PAD + MLA_NOPE:(hh + 1) * MLA_QK_PAD]
            qb_ref[:, c0:c0 + MLA_NOPE] = (nope * scale_b).astype(BF16)
            qb_ref[:, c0 + MLA_NOPE:c0 + MLA_QK_PAD] = (
                _rope_lanes(rp, cb, sbl, sbh, MLA_ROPE // 2) * scale_b).astype(BF16)
    for cc in range(B_WIDTH // chunk):
        kn = jnp.dot(ckv, wuk_ref[:, cc * chunk:(cc + 1) * chunk], preferred_element_type=F32)
        for hh in range(chunk // MLA_NOPE):
            head = cc * (chunk // MLA_NOPE) + hh
            kb_ref[:, head * MLA_QK_PAD:head * MLA_QK_PAD + MLA_NOPE] = (
                kn[:, hh * MLA_NOPE:(hh + 1) * MLA_NOPE].astype(BF16))
            kb_ref[:, head * MLA_QK_PAD + MLA_NOPE:(head + 1) * MLA_QK_PAD] = kr
    vbt_ref[...] = lax.dot_general(wuvt_ref[...], ckv, _NT, preferred_element_type=F32).astype(BF16)


def _in_proj(x2, mod3, w_main, wvt, wuq_r, wuk, wuvt, gq, gkv, tabs_a, tabs_b, seq):
    t, d = x2.shape
    tm = TM_PROJ
    per_b = seq // tm
    row = lambda i: (i, 0)
    col = lambda i: (0, i)
    tab = lambda i: (i % per_b, 0)
    mod_spec = lambda k: pl.BlockSpec((1, 1, d), lambda i: ((i // per_b) * 6 + k, 0, 0))
    qk_b = B_HEADS * MLA_QK_PAD
    out_specs = [pl.BlockSpec((tm, A_WIDTH), row), pl.BlockSpec((tm, A_WIDTH), row),
                 pl.BlockSpec((A_WIDTH, tm), col),
                 pl.BlockSpec((tm, qk_b), row), pl.BlockSpec((tm, qk_b), row),
                 pl.BlockSpec((B_WIDTH, tm), col)]
    out_shape = [jax.ShapeDtypeStruct(s, BF16) for s in
                 ((t, A_WIDTH), (t, A_WIDTH), (A_WIDTH, t), (t, qk_b), (t, qk_b), (B_WIDTH, t))]
    return pl.pallas_call(
        functools.partial(_in_proj_kernel, scale_a=HEAD_DIM ** -0.5 * LOG2E,
                          scale_b=(MLA_NOPE + MLA_ROPE) ** -0.5 * LOG2E),
        grid=(t // tm,),
        in_specs=[pl.BlockSpec((tm, d), row), mod_spec(1), mod_spec(0),
                  _resident(w_main.shape), _resident(wvt.shape), _resident(wuq_r.shape), _resident(wuk.shape),
                  _resident(wuvt.shape), _resident(gq.shape), _resident(gkv.shape)]
                 + [pl.BlockSpec((tm, V7X_LANES), tab)] * 6,
        out_specs=out_specs,
        out_shape=out_shape,
        compiler_params=_params("arbitrary"),
        name="in_proj",
    )(x2, mod3, mod3, w_main, wvt, wuq_r, wuk, wuvt, gq, gkv, *tabs_a, *tabs_b)


def _attn_kernel(q_ref, k_ref, vt_ref, bias_ref, o_ref, *, tq, bias_all_keys):
    seq = q_ref.shape[0]
    nq = seq // tq
    nb = bias_ref.shape[0]

    def key_parts(qi):
        n = (qi + 1) * tq
        if bias_all_keys:
            return [(0, n, True)]
        return ([(0, n - tq, False)] if qi > 0 else []) + [(n - tq, n, True)]

    def score_tiles(qi):
        q = q_ref[qi * tq:(qi + 1) * tq, :]
        tiles = []
        for k0, k1, biased in key_parts(qi):
            st = lax.dot_general(k_ref[k0:k1, :], q, _NT, preferred_element_type=F32)
            tiles.append(st + bias_ref[nb - (k1 - k0):nb, :] if biased else st)
        return tiles

    ahead = score_tiles(0)
    for qi in range(nq):
        parts, scores = key_parts(qi), ahead
        if qi + 1 < nq:
            ahead = score_tiles(qi + 1)
        m = jnp.max(scores[0], axis=0, keepdims=True)
        for st in scores[1:]:
            m = jnp.maximum(m, jnp.max(st, axis=0, keepdims=True))
        l = jnp.zeros_like(m)
        o_t = jnp.zeros((vt_ref.shape[0], tq), F32)
        for (k0, k1, _), st in zip(parts, scores):
            p = jnp.exp2(st - m)
            l = l + jnp.sum(p, axis=0, keepdims=True)
            o_t = o_t + jnp.dot(vt_ref[:, k0:k1], p.astype(BF16), preferred_element_type=F32)
        o_ref[qi * tq:(qi + 1) * tq, :] = (o_t * (1.0 / l)).T


def _attention(q, k, vt, bias, *, seq, heads, ek, bias_all_keys):
    t = q.shape[0]
    bsz = t // seq
    ev = vt.shape[0] // heads
    return pl.pallas_call(
        functools.partial(_attn_kernel, tq=TQ_ATTN, bias_all_keys=bias_all_keys),
        grid=(bsz, heads),
        in_specs=[pl.BlockSpec((seq, ek), lambda b, h: (b, h)),
                  pl.BlockSpec((seq, ek), lambda b, h: (b, h)),
                  pl.BlockSpec((ev, seq), lambda b, h: (h, b)),
                  _resident(bias.shape)],
        out_specs=pl.BlockSpec((seq, ev), lambda b, h: (b, h)),
        out_shape=jax.ShapeDtypeStruct((t, heads * ev), F32),
        compiler_params=_params("arbitrary", "arbitrary"),
        name="attention",
    )(q, k, vt, bias)


def _bias_tables(tq, seq):
    d = ((seq - tq) + jnp.arange(tq, dtype=jnp.int32)[None, :] - jnp.arange(seq, dtype=jnp.int32)[:, None])
    ok = d >= 0
    mult = ((d <= 128).astype(F32) + ((d % 4 == 0) & (d <= 512)).astype(F32)
            + ((d % 16 == 0) & (d <= 2048)).astype(F32))
    mix = jnp.where(ok & (mult > 0), jnp.log2(jnp.maximum(mult, 1.0)), NEG_BIG).astype(F32)
    causal = jnp.where(ok[seq - tq:], 0.0, NEG_BIG).astype(F32)
    return mix, causal


def _post_attn_kernel(oa_ref, ob_ref, x_ref, g1_ref, sc_ref, sh_ref, ga_ref, gb_ref, wo_ref, l1g_ref, l1b_ref,
                      wpq_ref, sk_ref, x1_ref, h2_ref, st_ref):
    na = _rms_norm(oa_ref[...], ga_ref[...]).astype(BF16)
    nb = _rms_norm(ob_ref[...], gb_ref[...]).astype(BF16)
    mix = (jnp.dot(na, wo_ref[0:A_WIDTH, :], preferred_element_type=F32)
           + jnp.dot(nb, wo_ref[A_WIDTH:A_WIDTH + B_WIDTH, :], preferred_element_type=F32))
    x1 = _layer_norm(DN_ALPHA * x_ref[...] + g1_ref[0] * mix) * l1g_ref[...] + l1b_ref[...]
    x1_ref[...] = x1
    h2_f32 = _layer_norm(x1) * (1.0 + sc_ref[0]) + sh_ref[0]
    h2_ref[...] = _to_f8(h2_f32)
    q = jnp.dot(h2_f32.astype(BF16), wpq_ref[...], preferred_element_type=F32).astype(BF16)
    for hh in range(PEER_HEADS):
        for half in range(2):
            r0 = (hh * 2 + half) * PEER_N_KEYS
            st_ref[r0:r0 + PEER_N_KEYS, :] = lax.dot_general(
                sk_ref[half], q[:, r0:r0 + PEER_N_KEYS], _NT, preferred_element_type=F32)


def _post_attn(oa, ob, x2, mod3, ga, gb, wo, l1g, l1b, wpq, sk, seq):
    t, d = x2.shape
    tm = TM_PROJ
    per_b = seq // tm
    row = lambda i: (i, 0)
    mod_spec = lambda k: pl.BlockSpec((1, 1, d), lambda i: ((i // per_b) * 6 + k, 0, 0))
    n_sc = 2 * PEER_HEADS * PEER_N_KEYS
    return pl.pallas_call(
        _post_attn_kernel,
        grid=(t // tm,),
        in_specs=[pl.BlockSpec((tm, A_WIDTH), row), pl.BlockSpec((tm, B_WIDTH), row), pl.BlockSpec((tm, d), row),
                  mod_spec(2), mod_spec(4), mod_spec(3),
                  _resident(ga.shape), _resident(gb.shape), _resident(wo.shape),
                  _resident(l1g.shape), _resident(l1b.shape), _resident(wpq.shape), _resident(sk.shape)],
        out_specs=[pl.BlockSpec((tm, d), row), pl.BlockSpec((tm, d), row),
                   pl.BlockSpec((n_sc, tm), lambda i: (0, i))],
        out_shape=[jax.ShapeDtypeStruct((t, d), F32), jax.ShapeDtypeStruct((t, d), F8),
                   jax.ShapeDtypeStruct((n_sc, t), F32)],
        compiler_params=_params("arbitrary"),
        name="post_attn",
    )(oa, ob, x2, mod3, mod3, mod3, ga, gb, wo, l1g, l1b, wpq, sk)


_GROUPS = PEER_N_KEYS // V7X_SUBLANES


def _oddeven_merge_sort_pairs(n):
    pairs = []
    p = 1
    while p < n:
        k = p
        while k >= 1:
            for j in range(k % p, n - k, 2 * k):
                for i in range(min(k, n - j - k)):
                    if (i + j) // (2 * p) == (i + j + k) // (2 * p):
                        pairs.append((i + j, i + j + k))
            k //= 2
        p *= 2
    return pairs


_SORT16 = _oddeven_merge_sort_pairs(_GROUPS)


def _sublane_max(x):
    for shift in (4, 2, 1):
        x = jnp.maximum(x, pltpu.roll(x, shift, 0))
    return x


def _sorted_top16(groups):
    c = list(groups)
    for i, j in _SORT16:
        c[i], c[j] = jnp.maximum(c[i], c[j]), jnp.minimum(c[i], c[j])
    out = []
    for r in range(PEER_TOPK):
        m = _sublane_max(c[0])
        out.append(m)
        if r + 1 < PEER_TOPK:
            hit = c[0] == m
            c = [jnp.where(hit, c[k + 1], c[k]) for k in range(len(c) - 1)]
    return out


def _count_leading(pred, v):
    p8 = pred(v[7])
    p4 = pred(jnp.where(p8, v[11], v[3]))
    p2 = pred(jnp.where(p8, jnp.where(p4, v[13], v[9]), jnp.where(p4, v[5], v[1])))
    hi = jnp.where(p4, jnp.where(p2, v[14], v[12]), jnp.where(p2, v[10], v[8]))
    lo = jnp.where(p4, jnp.where(p2, v[6], v[4]), jnp.where(p2, v[2], v[0]))
    p1 = pred(jnp.where(p8, hi, lo))
    p16 = pred(v[15])
    one, zero = jnp.float32(1.0), jnp.float32(0.0)
    return (jnp.where(p8, 8.0, zero) + jnp.where(p4, 4.0, zero) + jnp.where(p2, 2.0, zero)
            + jnp.where(p1, one, zero) + jnp.where(p16, one, zero))


def _peer_topk_kernel(st_ref, e1_ref, cnt_ref, e2_ref, r2_ref):
    tl = st_ref.shape[1]
    sub = lax.broadcasted_iota(jnp.int32, (V7X_SUBLANES, tl), 0)

    def by_sublane(rows):
        out = rows[0]
        for j in range(1, V7X_SUBLANES):
            out = jnp.where(sub == j, rows[j], out)
        return out

    for hh in range(PEER_HEADS):
        r0 = hh * 2 * PEER_N_KEYS
        g1 = [st_ref[r0 + V7X_SUBLANES * k:r0 + V7X_SUBLANES * (k + 1), :] for k in range(_GROUPS)]
        g2 = [st_ref[r0 + PEER_N_KEYS + V7X_SUBLANES * k:r0 + PEER_N_KEYS + V7X_SUBLANES * (k + 1), :]
              for k in range(_GROUPS)]
        v1 = _sorted_top16(g1)
        v2 = _sorted_top16(g2)
        v2_lo, v2_hi, v1_hi = by_sublane(v2[:8]), by_sublane(v2[8:]), by_sublane(v1[8:])
        cand = [v1[0] + v2_lo, v1[0] + v2_hi, v1[1] + v2_lo]
        for a in range(2, 8):
            cand.append(jnp.where(sub < PEER_TOPK // (a + 1), v1[a] + v2_lo, -jnp.inf))
        cand.append(v1_hi + v2[0])
        cmax = v1[0] + v2[0]
        z = jnp.zeros_like(cmax)
        tau = cmax
        for r in range(PEER_TOPK):
            tau = cand[0]
            for blk in cand[1:]:
                tau = jnp.maximum(tau, blk)
            tau = _sublane_max(tau)
            z = z + jnp.exp(tau - cmax)
            if r + 1 < PEER_TOPK:
                cand = [jnp.where(blk == tau, -jnp.inf, blk) for blk in cand]
        inv_z = 1.0 / z
        for k in range(_GROUPS):
            rows = slice(hh * PEER_N_KEYS + V7X_SUBLANES * k, hh * PEER_N_KEYS + V7X_SUBLANES * (k + 1))
            x1 = g1[k]
            e1_ref[rows, :] = jnp.exp(x1 - v1[0]) * inv_z
            cnt_ref[rows, :] = _count_leading(lambda vb: x1 + vb >= tau, v2)
        for k in range(0, _GROUPS, 2):
            rows = slice(hh * PEER_N_KEYS + V7X_SUBLANES * k, hh * PEER_N_KEYS + V7X_SUBLANES * (k + 2))
            e2_ref[rows, :] = jnp.concatenate(
                [jnp.exp(g2[k + j] - v2[0]) for j in range(2)], axis=0).astype(BF16)
            r2_ref[rows, :] = jnp.concatenate(
                [_count_leading(lambda vb, x=g2[k + j]: vb > x, v2) for j in range(2)], axis=0).astype(BF16)


def _peer_topk(st):
    rows, t = st.shape
    tl = TL_TOPK
    n = PEER_HEADS * PEER_N_KEYS
    col = lambda i: (0, i)
    return pl.pallas_call(
        _peer_topk_kernel,
        grid=(t // tl,),
        in_specs=[pl.BlockSpec((rows, tl), col)],
        out_specs=[pl.BlockSpec((n, tl), col)] * 4,
        out_shape=[jax.ShapeDtypeStruct((n, t), F32), jax.ShapeDtypeStruct((n, t), F32),
                   jax.ShapeDtypeStruct((n, t), BF16), jax.ShapeDtypeStruct((n, t), BF16)],
        compiler_params=_params("arbitrary"),
        name="peer_topk",
    )(st)


def _gelu(x):
    half = 0.5 * x
    return half + half * lax.erf(x * (2.0 ** -0.5))


def _row_bf16(ref, row, cols, rows_out):
    r = jnp.broadcast_to(ref[pl.ds(row, 1), cols], (V7X_BF16_SUBLANES, cols.stop - cols.start)).astype(BF16)
    return jnp.concatenate([r] * (rows_out // V7X_BF16_SUBLANES), axis=0)


def _peer_dense_kernel(h2_ref, e1_ref, cnt_ref, e2_ref, r2_ref, u_ref, vt_ref, o_ref, a_sc, w_sc, g_sc, *, te, ts):
    e = pl.program_id(1)
    n_i1 = te // PEER_N_KEYS
    tm = h2_ref.shape[0]

    @pl.when(e == 0)
    def _():
        o_ref[...] = jnp.zeros(o_ref.shape, F32)

    chains = [slice(s * ts, (s + 1) * ts) for s in range(tm // ts)]
    d_model = vt_ref.shape[0]
    n_piece = te // PIECE_PEER
    per_piece = PIECE_PEER // PEER_N_KEYS
    d_piece = d_model // n_piece

    def first_piece(c, k):
        rows = slice(k * PIECE_PEER, (k + 1) * PIECE_PEER)
        a_sc[c, rows, :] = lax.dot_general(u_ref[rows, :], h2_ref[chains[c], :], _NT,
                                           preferred_element_type=F32)

    def gate_tile(c, ii):
        i1 = e * n_i1 + ii
        w = jnp.zeros((PEER_N_KEYS, ts), BF16)
        for hh in range(PEER_HEADS):
            rows = slice(hh * PEER_N_KEYS, (hh + 1) * PEER_N_KEYS)
            e1 = _row_bf16(e1_ref, hh * PEER_N_KEYS + i1, chains[c], PEER_N_KEYS)
            cnt = _row_bf16(cnt_ref, hh * PEER_N_KEYS + i1, chains[c], PEER_N_KEYS)
            w = w + jnp.where(r2_ref[rows, chains[c]] < cnt, e2_ref[rows, chains[c]], jnp.zeros_like(w)) * e1
        w_sc[c, ii * PEER_N_KEYS:(ii + 1) * PEER_N_KEYS, :] = w

    def act_tile(c, ii):
        rows = slice(ii * PEER_N_KEYS, (ii + 1) * PEER_N_KEYS)
        g_sc[c, rows, :] = _to_f8(_gelu(a_sc[c, rows, :].astype(BF16)) * w_sc[c, rows, :])

    def second_piece(c, k):
        rows = slice(k * d_piece, (k + 1) * d_piece)
        o_ref[rows, chains[c]] += jnp.dot(vt_ref[rows, :], g_sc[c], preferred_element_type=F32)

    def tiles_of(k):
        return range(k * per_piece, (k + 1) * per_piece)

    for c in range(len(chains)):
        for k in range(n_piece):
            first_piece(c, k)
            for ii in tiles_of(k):
                if c == 0:
                    gate_tile(0, ii)
                if c == 1 or len(chains) == 1:
                    act_tile(0, ii)
    for c in range(len(chains)):
        for k in range(n_piece):
            second_piece(c, k)
            if c + 1 < len(chains):
                for ii in tiles_of(k):
                    gate_tile(c + 1, ii)
        if c + 1 < len(chains):
            for ii in range(n_i1):
                act_tile(c + 1, ii)


def _peer_dense(h2, e1, cnt, e2, r2, u_bf, vt_bf):
    t, d = h2.shape
    n_e = u_bf.shape[0]
    n = e1.shape[0]
    tm, te = TM_PEER, TE_PEER
    n_chain = tm // TS_PEER
    tok = lambda i, e: (0, i)
    once = pl.Buffered(1)
    return pl.pallas_call(
        functools.partial(_peer_dense_kernel, te=te, ts=TS_PEER),
        grid=(t // tm, n_e // te),
        in_specs=[pl.BlockSpec((tm, d), lambda i, e: (i, 0)),
                  pl.BlockSpec((n, tm), tok), pl.BlockSpec((n, tm), tok),
                  pl.BlockSpec((n, tm), tok), pl.BlockSpec((n, tm), tok),
                  pl.BlockSpec((te, d), lambda i, e: (e, 0)),
                  pl.BlockSpec((d, te), lambda i, e: (0, e))],
        out_specs=pl.BlockSpec((d, tm), tok, pipeline_mode=once),
        out_shape=jax.ShapeDtypeStruct((d, t), F32),
        scratch_shapes=[pltpu.VMEM((n_chain, te, TS_PEER), F32), pltpu.VMEM((n_chain, te, TS_PEER), BF16),
                        pltpu.VMEM((n_chain, te, TS_PEER), F8)],
        compiler_params=_params("arbitrary", "arbitrary"),
        name="peer_dense",
    )(h2, e1, cnt, e2, r2, u_bf, vt_bf)


def _table_t_kernel(x_ref, o_ref):
    o_ref[...] = _to_f8(x_ref[...].T)


def _table_t(x):
    rows, cols = x.shape
    tr = 512
    return pl.pallas_call(
        _table_t_kernel,
        grid=(rows // tr,),
        in_specs=[pl.BlockSpec((tr, cols), lambda i: (i, 0))],
        out_specs=pl.BlockSpec((cols, tr), lambda i: (0, i)),
        out_shape=jax.ShapeDtypeStruct((cols, rows), F8),
        compiler_params=_params("arbitrary"),
        name="table_t",
    )(x)


def _final_ln_kernel(x1_ref, ft_ref, g2_ref, lg_ref, lb_ref, o_ref):
    o_ref[...] = _layer_norm(DN_ALPHA * x1_ref[...] + g2_ref[0] * ft_ref[...].T) * lg_ref[...] + lb_ref[...]


def _final_ln(x1, ffn_t, mod3, lg, lb, seq):
    t, d = x1.shape
    tm = TM_FINAL
    per_b = seq // tm
    row = lambda i: (i, 0)
    return pl.pallas_call(
        _final_ln_kernel,
        grid=(t // tm,),
        in_specs=[pl.BlockSpec((tm, d), row), pl.BlockSpec((d, tm), lambda i: (0, i)),
                  pl.BlockSpec((1, 1, d), lambda i: ((i // per_b) * 6 + 5, 0, 0)),
                  _resident(lg.shape), _resident(lb.shape)],
        out_specs=pl.BlockSpec((tm, d), row),
        out_shape=jax.ShapeDtypeStruct((t, d), F32),
        compiler_params=_params("arbitrary"),
        name="final_ln",
    )(x1, ffn_t, mod3, lg, lb)


def _rope_tables(seq, rot_dim):
    half = rot_dim // 2
    pos = jnp.arange(seq, dtype=F32)
    inv = ROPE_THETA ** (-jnp.arange(half, dtype=F32) * 2.0 / rot_dim)
    ang = pos[:, None] * inv[None, :]
    cos, sin = jnp.cos(ang), jnp.sin(ang)
    pad = V7X_LANES - rot_dim
    ones = jnp.ones((seq, pad), F32)
    zeros = jnp.zeros((seq, pad), F32)
    zh = jnp.zeros((seq, half), F32)
    return (jnp.concatenate([cos, cos, ones], 1),
            jnp.concatenate([-sin, zh, zeros], 1),
            jnp.concatenate([zh, sin, zeros], 1))


def _layer(x2, mod3, seq, w_in, g_q_lat, g_kv_lat, w_uq, w_uk, w_uv, g_out_a, g_out_b, w_o,
           ln1_g, ln1_b, w_pq, sub_key_1, sub_key_2, u_table, v_table, ln2_g, ln2_b):
    d = x2.shape[1]
    w_bf = w_in.astype(BF16)
    w_main = jnp.concatenate([w_bf[:, :2 * A_WIDTH], w_bf[:, 3 * A_WIDTH:]], axis=1)
    w_main = jnp.pad(w_main, ((0, 0), (0, (-w_main.shape[1]) % V7X_LANES)))
    wvt = w_bf[:, 2 * A_WIDTH:3 * A_WIDTH].T
    pad_q = MLA_QK_PAD - MLA_NOPE - MLA_ROPE
    wuq_r = jnp.pad(w_uq.astype(BF16).reshape(MLA_RANK, B_HEADS, MLA_NOPE + MLA_ROPE),
                    ((0, 0), (0, 0), (0, pad_q))).reshape(MLA_RANK, B_HEADS * MLA_QK_PAD)
    tabs_a = _rope_tables(seq, A_ROT_DIM)
    tabs_b = _rope_tables(seq, MLA_ROPE)
    qa, ka, vat, qb, kb, vbt = _in_proj(
        x2, mod3, w_main, wvt, wuq_r, w_uk.astype(BF16), w_uv.astype(BF16).T,
        g_q_lat.reshape(1, -1), g_kv_lat.reshape(1, -1), tabs_a, tabs_b, seq)

    bias_mix, bias_causal = _bias_tables(TQ_ATTN, seq)
    oa = _attention(qa, ka, vat, bias_mix, seq=seq, heads=A_HEADS, ek=HEAD_DIM, bias_all_keys=True)
    ob = _attention(qb, kb, vbt, bias_causal, seq=seq, heads=B_HEADS, ek=MLA_QK_PAD, bias_all_keys=False)

    sk = jnp.stack([sub_key_1, sub_key_2]).astype(BF16)
    x1, h2, st = _post_attn(oa, ob, x2, mod3, g_out_a.reshape(1, -1), g_out_b.reshape(1, -1),
                            w_o.astype(BF16), ln1_g.reshape(1, -1), ln1_b.reshape(1, -1),
                            w_pq.astype(BF16), sk, seq)
    e1, cnt, e2, r2 = _peer_topk(st)
    ffn_t = _peer_dense(h2, e1, cnt, e2, r2, _to_f8(u_table), _table_t(v_table))
    return _final_ln(x1, ffn_t, mod3, ln2_g.reshape(1, d), ln2_b.reshape(1, d), seq)


def kernel(x, c, w_ada, b_ada, w_in, g_q_lat, g_kv_lat, w_uq, w_uk, w_uv, g_out_a, g_out_b, w_o, ln1_g, ln1_b,
           w_pq, sub_key_1, sub_key_2, u_table, v_table, ln2_g, ln2_b):
    bsz, seq, d = x.shape
    x2 = x.reshape(bsz * seq, d)
    for l in range(w_ada.shape[0]):
        mod3 = _ada_mod(c, w_ada[l], b_ada[l]).reshape(bsz * 6, 1, d)
        x2 = _layer(x2, mod3, seq, w_in[l], g_q_lat[l], g_kv_lat[l], w_uq[l], w_uk[l], w_uv[l],
                    g_out_a[l], g_out_b[l], w_o[l], ln1_g[l], ln1_b[l], w_pq[l], sub_key_1[l], sub_key_2[l],
                    u_table[l], v_table[l], ln2_g[l], ln2_b[l])
    return x2.reshape(bsz, seq, d)
```

```python
import functools
import math

import jax
import jax.numpy as jnp
from jax import lax
from jax.experimental import pallas as pl
from jax.experimental.pallas import tpu as pltpu

F32 = jnp.float32
BF16 = jnp.bfloat16
EXPERT_DT = BF16

D_MODEL = 2048
HEAD_DIM = 128
A_HEADS = 8
A_WIDTH = A_HEADS * HEAD_DIM
A_ROT_DIM = 32
ROPE_THETA = 500000.0
B_HEADS = 8
MLA_RANK = 512
MLA_NOPE = 128
MLA_ROPE = 64
MLA_V = 128
B_WIDTH = B_HEADS * MLA_V
MLA_QK_PAD = 256
PEER_HEADS = 8
PEER_N_KEYS = 128
PEER_TOPK = 16
DN_ALPHA = 2.0 ** 0.25

V7X_LANES = 128
V7X_SUBLANES = 8
V7X_BF16_SUBLANES = 16
V7X_VMEM_LIMIT = 56 * 1024 * 1024

TM_PROJ = 256
TQ_ATTN = 512
TL_TOPK = 256
TM_PEER = 1024
TS_PEER = 512
TE_PEER = 1024
PIECE_PEER = 256
TM_FINAL = 512

NEG_BIG = -1e30
LOG2E = math.log2(math.e)

_NT = (((1,), (1,)), ((), ()))

_C_QA, _C_KA, _C_CQ, _C_CKV, _C_KR = 0, A_WIDTH, 2 * A_WIDTH, 2 * A_WIDTH + MLA_RANK, 2 * A_WIDTH + 2 * MLA_RANK


def _params(*sem):
    return pltpu.CompilerParams(dimension_semantics=sem, vmem_limit_bytes=V7X_VMEM_LIMIT)


def _resident(shape):
    nd = len(shape)
    return pl.BlockSpec(shape, lambda *_: (0,) * nd, pipeline_mode=pl.Buffered(1))


def _layer_norm(x, eps=1e-5):
    mu = jnp.mean(x, axis=-1, keepdims=True)
    xc = x - mu
    var = jnp.mean(xc * xc, axis=-1, keepdims=True)
    return xc * lax.rsqrt(var + eps)


def _rms_norm(x, g, eps=1e-6):
    return x * lax.rsqrt(jnp.mean(x * x, axis=-1, keepdims=True) + eps) * g


def _rope_lanes(x, cos, sin_lo, sin_hi, half):
    return (x * cos + pltpu.roll(x, V7X_LANES - half, 1) * sin_lo + pltpu.roll(x, half, 1) * sin_hi)


def _ada_kernel(c_ref, w_ref, b_ref, o_ref):
    c = c_ref[...]
    act = (c / (1.0 + jnp.exp(-c))).astype(BF16)
    o_ref[...] = jnp.dot(act, w_ref[...].astype(BF16), preferred_element_type=F32) + b_ref[...]


def _ada_mod(c, w_ada, b_ada):
    bsz, d = c.shape
    n = w_ada.shape[1]
    tn = 1024
    return pl.pallas_call(
        _ada_kernel,
        grid=(n // tn,),
        in_specs=[pl.BlockSpec((bsz, d), lambda j: (0, 0)),
                  pl.BlockSpec((d, tn), lambda j: (0, j)),
                  pl.BlockSpec((1, tn), lambda j: (0, j))],
        out_specs=pl.BlockSpec((bsz, tn), lambda j: (0, j)),
        out_shape=jax.ShapeDtypeStruct((bsz, n), F32),
        compiler_params=_params("arbitrary"),
        name="ada_mod",
    )(c, w_ada, b_ada.reshape(1, n))


def _in_proj_kernel(x_ref, sc_ref, sh_ref, w_ref, wvt_ref, wuq_ref, wuk_ref, wuvt_ref, gq_ref, gkv_ref,
                    ca_ref, sal_ref, sah_ref, cb_ref, sbl_ref, sbh_ref,
                    qa_ref, ka_ref, vat_ref, qb_ref, kb_ref, vbt_ref, *, scale_a, scale_b):
    h = (_layer_norm(x_ref[...]) * (1.0 + sc_ref[0]) + sh_ref[0]).astype(BF16)

    def proj(c0, width):
        return jnp.dot(h, w_ref[:, c0:c0 + width], preferred_element_type=F32)

    ca, sal, sah = ca_ref[...], sal_ref[...], sah_ref[...]
    cb, sbl, sbh = cb_ref[...], sbl_ref[...], sbh_ref[...]
    chunk = 512
    for c_base, out_ref, scale in ((_C_QA, qa_ref, scale_a), (_C_KA, ka_ref, None)):
        for cc in range(A_WIDTH // chunk):
            acc = proj(c_base + cc * chunk, chunk)
            for hh in range(chunk // HEAD_DIM):
                r = _rope_lanes(acc[:, hh * HEAD_DIM:(hh + 1) * HEAD_DIM], ca, sal, sah, A_ROT_DIM // 2)
                if scale is not None:
                    r = r * scale
                c0 = cc * chunk + hh * HEAD_DIM
                out_ref[:, c0:c0 + HEAD_DIM] = r.astype(BF16)
    vat_ref[...] = lax.dot_general(wvt_ref[...], h, _NT, preferred_element_type=F32).astype(BF16)
    cq = _rms_norm(proj(_C_CQ, MLA_RANK), gq_ref[...]).astype(BF16)
    ckv = _rms_norm(proj(_C_CKV, MLA_RANK), gkv_ref[...]).astype(BF16)
    kr = _rope_lanes(proj(_C_KR, V7X_LANES), cb, sbl, sbh, MLA_ROPE // 2).astype(BF16)
    for cc in range(B_HEADS * MLA_QK_PAD // chunk):
        qq = jnp.dot(cq, wuq_ref[:, cc * chunk:(cc + 1) * chunk], preferred_element_type=F32)
        for hh in range(chunk // MLA_QK_PAD):
            c0 = cc * chunk + hh * MLA_QK_PAD
            nope = qq[:, hh * MLA_QK_PAD:hh * MLA_QK_PAD + MLA_NOPE]
            rp = qq[:, hh * MLA_QK_PAD + MLA_NOPE:(hh + 1) * MLA_QK_PAD]
            qb_ref[:, c0:c0 + MLA_NOPE] = (nope * scale_b).astype(BF16)
            qb_ref[:, c0 + MLA_NOPE:c0 + MLA_QK_PAD] = (
                _rope_lanes(rp, cb, sbl, sbh, MLA_ROPE // 2) * scale_b).astype(BF16)
    for cc in range(B_WIDTH // chunk):
        kn = jnp.dot(ckv, wuk_ref[:, cc * chunk:(cc + 1) * chunk], preferred_element_type=F32)
        for hh in range(chunk // MLA_NOPE):
            head = cc * (chunk // MLA_NOPE) + hh
            kb_ref[:, head * MLA_QK_PAD:head * MLA_QK_PAD + MLA_NOPE] = (
                kn[:, hh * MLA_NOPE:(hh + 1) * MLA_NOPE].astype(BF16))
            kb_ref[:, head * MLA_QK_PAD + MLA_NOPE:(head + 1) * MLA_QK_PAD] = kr
    vbt_ref[...] = lax.dot_general(wuvt_ref[...], ckv, _NT, preferred_element_type=F32).astype(BF16)


def _in_proj(x2, mod3, w_main, wvt, wuq_r, wuk, wuvt, gq, gkv, tabs_a, tabs_b, seq):
    t, d = x2.shape
    tm = TM_PROJ
    per_b = seq // tm
    row = lambda i: (i, 0)
    col = lambda i: (0, i)
    tab = lambda i: (i % per_b, 0)
    mod_spec = lambda k: pl.BlockSpec((1, 1, d), lambda i: ((i // per_b) * 6 + k, 0, 0))
    qk_b = B_HEADS * MLA_QK_PAD
    out_specs = [pl.BlockSpec((tm, A_WIDTH), row), pl.BlockSpec((tm, A_WIDTH), row),
                 pl.BlockSpec((A_WIDTH, tm), col),
                 pl.BlockSpec((tm, qk_b), row), pl.BlockSpec((tm, qk_b), row),
                 pl.BlockSpec((B_WIDTH, tm), col)]
    out_shape = [jax.ShapeDtypeStruct(s, BF16) for s in
                 ((t, A_WIDTH), (t, A_WIDTH), (A_WIDTH, t), (t, qk_b), (t, qk_b), (B_WIDTH, t))]
    return pl.pallas_call(
        functools.partial(_in_proj_kernel, scale_a=HEAD_DIM ** -0.5 * LOG2E,
                          scale_b=(MLA_NOPE + MLA_ROPE) ** -0.5 * LOG2E),
        grid=(t // tm,),
        in_specs=[pl.BlockSpec((tm, d), row), mod_spec(1), mod_spec(0),
                  _resident(w_main.shape), _resident(wvt.shape), _resident(wuq_r.shape), _resident(wuk.shape),
                  _resident(wuvt.shape), _resident(gq.shape), _resident(gkv.shape)]
                 + [pl.BlockSpec((tm, V7X_LANES), tab)] * 6,
        out_specs=out_specs,
        out_shape=out_shape,
        compiler_params=_params("arbitrary"),
        name="in_proj",
    )(x2, mod3, mod3, w_main, wvt, wuq_r, wuk, wuvt, gq, gkv, *tabs_a, *tabs_b)


def _attn_kernel(q_ref, k_ref, vt_ref, bias_ref, o_ref, *, tq, bias_all_keys):
    seq = q_ref.shape[0]
    nq = seq // tq
    nb = bias_ref.shape[0]

    def key_parts(qi):
        n = (qi + 1) * tq
        if bias_all_keys:
            return [(0, n, True)]
        return ([(0, n - tq, False)] if qi > 0 else []) + [(n - tq, n, True)]

    def score_tiles(qi):
        q = q_ref[qi * tq:(qi + 1) * tq, :]
        tiles = []
        for k0, k1, biased in key_parts(qi):
            st = lax.dot_general(k_ref[k0:k1, :], q, _NT, preferred_element_type=F32)
            tiles.append(st + bias_ref[nb - (k1 - k0):nb, :] if biased else st)
        return tiles

    ahead = score_tiles(0)
    for qi in range(nq):
        parts, scores = key_parts(qi), ahead
        if qi + 1 < nq:
            ahead = score_tiles(qi + 1)
        m = jnp.max(scores[0], axis=0, keepdims=True)
        for st in scores[1:]:
            m = jnp.maximum(m, jnp.max(st, axis=0, keepdims=True))
        l = jnp.zeros_like(m)
        o_t = jnp.zeros((vt_ref.shape[0], tq), F32)
        for (k0, k1, _), st in zip(parts, scores):
            p = jnp.exp2(st - m)
            l = l + jnp.sum(p, axis=0, keepdims=True)
            o_t = o_t + jnp.dot(vt_ref[:, k0:k1], p.astype(BF16), preferred_element_type=F32)
        o_ref[qi * tq:(qi + 1) * tq, :] = (o_t * (1.0 / l)).T


def _attention(q, k, vt, bias, *, seq, heads, ek, bias_all_keys):
    t = q.shape[0]
    bsz = t // seq
    ev = vt.shape[0] // heads
    return pl.pallas_call(
        functools.partial(_attn_kernel, tq=TQ_ATTN, bias_all_keys=bias_all_keys),
        grid=(bsz, heads),
        in_specs=[pl.BlockSpec((seq, ek), lambda b, h: (b, h)),
                  pl.BlockSpec((seq, ek), lambda b, h: (b, h)),
                  pl.BlockSpec((ev, seq), lambda b, h: (h, b)),
                  _resident(bias.shape)],
        out_specs=pl.BlockSpec((seq, ev), lambda b, h: (b, h)),
        out_shape=jax.ShapeDtypeStruct((t, heads * ev), F32),
        compiler_params=_params("arbitrary", "arbitrary"),
        name="attention",
    )(q, k, vt, bias)


def _bias_tables(tq, seq):
    d = ((seq - tq) + jnp.arange(tq, dtype=jnp.int32)[None, :] - jnp.arange(seq, dtype=jnp.int32)[:, None])
    ok = d >= 0
    mult = ((d <= 128).astype(F32) + ((d % 4 == 0) & (d <= 512)).astype(F32)
            + ((d % 16 == 0) & (d <= 2048)).astype(F32))
    mix = jnp.where(ok & (mult > 0), jnp.log2(jnp.maximum(mult, 1.0)), NEG_BIG).astype(F32)
    causal = jnp.where(ok[seq - tq:], 0.0, NEG_BIG).astype(F32)
    return mix, causal


def _post_attn_kernel(oa_ref, ob_ref, x_ref, g1_ref, sc_ref, sh_ref, ga_ref, gb_ref, wo_ref, l1g_ref, l1b_ref,
                      wpq_ref, sk_ref, x1_ref, h2_ref, st_ref):
    na = _rms_norm(oa_ref[...], ga_ref[...]).astype(BF16)
    nb = _rms_norm(ob_ref[...], gb_ref[...]).astype(BF16)
    mix = (jnp.dot(na, wo_ref[0:A_WIDTH, :], preferred_element_type=F32)
           + jnp.dot(nb, wo_ref[A_WIDTH:A_WIDTH + B_WIDTH, :], preferred_element_type=F32))
    x1 = _layer_norm(DN_ALPHA * x_ref[...] + g1_ref[0] * mix) * l1g_ref[...] + l1b_ref[...]
    x1_ref[...] = x1
    h2 = (_layer_norm(x1) * (1.0 + sc_ref[0]) + sh_ref[0]).astype(BF16)
    h2_ref[...] = h2.astype(EXPERT_DT)
    q = jnp.dot(h2, wpq_ref[...], preferred_element_type=F32).astype(BF16)
    for hh in range(PEER_HEADS):
        for half in range(2):
            r0 = (hh * 2 + half) * PEER_N_KEYS
            st_ref[r0:r0 + PEER_N_KEYS, :] = lax.dot_general(
                sk_ref[half], q[:, r0:r0 + PEER_N_KEYS], _NT, preferred_element_type=F32)


def _post_attn(oa, ob, x2, mod3, ga, gb, wo, l1g, l1b, wpq, sk, seq):
    t, d = x2.shape
    tm = TM_PROJ
    per_b = seq // tm
    row = lambda i: (i, 0)
    mod_spec = lambda k: pl.BlockSpec((1, 1, d), lambda i: ((i // per_b) * 6 + k, 0, 0))
    n_sc = 2 * PEER_HEADS * PEER_N_KEYS
    return pl.pallas_call(
        _post_attn_kernel,
        grid=(t // tm,),
        in_specs=[pl.BlockSpec((tm, A_WIDTH), row), pl.BlockSpec((tm, B_WIDTH), row), pl.BlockSpec((tm, d), row),
                  mod_spec(2), mod_spec(4), mod_spec(3),
                  _resident(ga.shape), _resident(gb.shape), _resident(wo.shape),
                  _resident(l1g.shape), _resident(l1b.shape), _resident(wpq.shape), _resident(sk.shape)],
        out_specs=[pl.BlockSpec((tm, d), row), pl.BlockSpec((tm, d), row),
                   pl.BlockSpec((n_sc, tm), lambda i: (0, i))],
        out_shape=[jax.ShapeDtypeStruct((t, d), F32), jax.ShapeDtypeStruct((t, d), EXPERT_DT),
                   jax.ShapeDtypeStruct((n_sc, t), F32)],
        compiler_params=_params("arbitrary"),
        name="post_attn",
    )(oa, ob, x2, mod3, mod3, mod3, ga, gb, wo, l1g, l1b, wpq, sk)


_GROUPS = PEER_N_KEYS // V7X_SUBLANES


def _oddeven_merge_sort_pairs(n):
    pairs = []
    p = 1
    while p < n:
        k = p
        while k >= 1:
            for j in range(k % p, n - k, 2 * k):
                for i in range(min(k, n - j - k)):
                    if (i + j) // (2 * p) == (i + j + k) // (2 * p):
                        pairs.append((i + j, i + j + k))
            k //= 2
        p *= 2
    return pairs


_SORT16 = _oddeven_merge_sort_pairs(_GROUPS)


def _sublane_max(x):
    for shift in (4, 2, 1):
        x = jnp.maximum(x, pltpu.roll(x, shift, 0))
    return x


def _sorted_top16(groups):
    c = list(groups)
    for i, j in _SORT16:
        c[i], c[j] = jnp.maximum(c[i], c[j]), jnp.minimum(c[i], c[j])
    out = []
    for r in range(PEER_TOPK):
        m = _sublane_max(c[0])
        out.append(m)
        if r + 1 < PEER_TOPK:
            hit = c[0] == m
            c = [jnp.where(hit, c[k + 1], c[k]) for k in range(len(c) - 1)]
    return out


def _count_leading(pred, v):
    p8 = pred(v[7])
    p4 = pred(jnp.where(p8, v[11], v[3]))
    p2 = pred(jnp.where(p8, jnp.where(p4, v[13], v[9]), jnp.where(p4, v[5], v[1])))
    hi = jnp.where(p4, jnp.where(p2, v[14], v[12]), jnp.where(p2, v[10], v[8]))
    lo = jnp.where(p4, jnp.where(p2, v[6], v[4]), jnp.where(p2, v[2], v[0]))
    p1 = pred(jnp.where(p8, hi, lo))
    p16 = pred(v[15])
    one, zero = jnp.float32(1.0), jnp.float32(0.0)
    return (jnp.where(p8, 8.0, zero) + jnp.where(p4, 4.0, zero) + jnp.where(p2, 2.0, zero)
            + jnp.where(p1, one, zero) + jnp.where(p16, one, zero))


def _peer_topk_kernel(st_ref, e1_ref, cnt_ref, e2_ref, r2_ref):
    tl = st_ref.shape[1]
    sub = lax.broadcasted_iota(jnp.int32, (V7X_SUBLANES, tl), 0)

    def by_sublane(rows):
        out = rows[0]
        for j in range(1, V7X_SUBLANES):
            out = jnp.where(sub == j, rows[j], out)
        return out

    for hh in range(PEER_HEADS):
        r0 = hh * 2 * PEER_N_KEYS
        g1 = [st_ref[r0 + V7X_SUBLANES * k:r0 + V7X_SUBLANES * (k + 1), :] for k in range(_GROUPS)]
        g2 = [st_ref[r0 + PEER_N_KEYS + V7X_SUBLANES * k:r0 + PEER_N_KEYS + V7X_SUBLANES * (k + 1), :]
              for k in range(_GROUPS)]
        v1 = _sorted_top16(g1)
        v2 = _sorted_top16(g2)
        v2_lo, v2_hi, v1_hi = by_sublane(v2[:8]), by_sublane(v2[8:]), by_sublane(v1[8:])
        cand = [v1[0] + v2_lo, v1[0] + v2_hi, v1[1] + v2_lo]
        for a in range(2, 8):
            cand.append(jnp.where(sub < PEER_TOPK // (a + 1), v1[a] + v2_lo, -jnp.inf))
        cand.append(v1_hi + v2[0])
        cmax = v1[0] + v2[0]
        z = jnp.zeros_like(cmax)
        tau = cmax
        for r in range(PEER_TOPK):
            tau = cand[0]
            for blk in cand[1:]:
                tau = jnp.maximum(tau, blk)
            tau = _sublane_max(tau)
            z = z + jnp.exp(tau - cmax)
            if r + 1 < PEER_TOPK:
                cand = [jnp.where(blk == tau, -jnp.inf, blk) for blk in cand]
        inv_z = 1.0 / z
        for k in range(_GROUPS):
            rows = slice(hh * PEER_N_KEYS + V7X_SUBLANES * k, hh * PEER_N_KEYS + V7X_SUBLANES * (k + 1))
            x1 = g1[k]
            e1_ref[rows, :] = jnp.exp(x1 - v1[0]) * inv_z
            cnt_ref[rows, :] = _count_leading(lambda vb: x1 + vb >= tau, v2)
        for k in range(0, _GROUPS, 2):
            rows = slice(hh * PEER_N_KEYS + V7X_SUBLANES * k, hh * PEER_N_KEYS + V7X_SUBLANES * (k + 2))
            e2_ref[rows, :] = jnp.concatenate(
                [jnp.exp(g2[k + j] - v2[0]) for j in range(2)], axis=0).astype(BF16)
            r2_ref[rows, :] = jnp.concatenate(
                [_count_leading(lambda vb, x=g2[k + j]: vb > x, v2) for j in range(2)], axis=0).astype(BF16)


def _peer_topk(st):
    rows, t = st.shape
    tl = TL_TOPK
    n = PEER_HEADS * PEER_N_KEYS
    col = lambda i: (0, i)
    return pl.pallas_call(
        _peer_topk_kernel,
        grid=(t // tl,),
        in_specs=[pl.BlockSpec((rows, tl), col)],
        out_specs=[pl.BlockSpec((n, tl), col)] * 4,
        out_shape=[jax.ShapeDtypeStruct((n, t), F32), jax.ShapeDtypeStruct((n, t), F32),
                   jax.ShapeDtypeStruct((n, t), BF16), jax.ShapeDtypeStruct((n, t), BF16)],
        compiler_params=_params("arbitrary"),
        name="peer_topk",
    )(st)


def _gelu(x):
    half = 0.5 * x
    return half + half * lax.erf(x * (2.0 ** -0.5))


def _row_bf16(ref, row, cols, rows_out):
    r = jnp.broadcast_to(ref[pl.ds(row, 1), cols], (V7X_BF16_SUBLANES, cols.stop - cols.start)).astype(BF16)
    return jnp.concatenate([r] * (rows_out // V7X_BF16_SUBLANES), axis=0)


def _peer_dense_kernel(h2_ref, e1_ref, cnt_ref, e2_ref, r2_ref, u_ref, vt_ref, o_ref, a_sc, w_sc, g_sc, *, te, ts):
    e = pl.program_id(1)
    n_i1 = te // PEER_N_KEYS
    tm = h2_ref.shape[0]

    @pl.when(e == 0)
    def _():
        o_ref[...] = jnp.zeros(o_ref.shape, F32)

    chains = [slice(s * ts, (s + 1) * ts) for s in range(tm // ts)]
    d_model = vt_ref.shape[0]
    n_piece = te // PIECE_PEER
    per_piece = PIECE_PEER // PEER_N_KEYS
    d_piece = d_model // n_piece

    def first_piece(c, k):
        rows = slice(k * PIECE_PEER, (k + 1) * PIECE_PEER)
        a_sc[c, rows, :] = lax.dot_general(u_ref[rows, :], h2_ref[chains[c], :], _NT,
                                           preferred_element_type=F32)

    def gate_tile(c, ii):
        i1 = e * n_i1 + ii
        w = jnp.zeros((PEER_N_KEYS, ts), BF16)
        for hh in range(PEER_HEADS):
            rows = slice(hh * PEER_N_KEYS, (hh + 1) * PEER_N_KEYS)
            e1 = _row_bf16(e1_ref, hh * PEER_N_KEYS + i1, chains[c], PEER_N_KEYS)
            cnt = _row_bf16(cnt_ref, hh * PEER_N_KEYS + i1, chains[c], PEER_N_KEYS)
            w = w + jnp.where(r2_ref[rows, chains[c]] < cnt, e2_ref[rows, chains[c]], jnp.zeros_like(w)) * e1
        w_sc[c, ii * PEER_N_KEYS:(ii + 1) * PEER_N_KEYS, :] = w

    def act_tile(c, ii):
        rows = slice(ii * PEER_N_KEYS, (ii + 1) * PEER_N_KEYS)
        g_sc[c, rows, :] = (_gelu(a_sc[c, rows, :].astype(BF16)) * w_sc[c, rows, :]).astype(EXPERT_DT)

    def second_piece(c, k):
        rows = slice(k * d_piece, (k + 1) * d_piece)
        o_ref[rows, chains[c]] += jnp.dot(vt_ref[rows, :], g_sc[c], preferred_element_type=F32)

    def tiles_of(k):
        return range(k * per_piece, (k + 1) * per_piece)

    for c in range(len(chains)):
        for k in range(n_piece):
            first_piece(c, k)
            for ii in tiles_of(k):
                if c == 0:
                    gate_tile(0, ii)
                if c == 1 or len(chains) == 1:
                    act_tile(0, ii)
    for c in range(len(chains)):
        for k in range(n_piece):
            second_piece(c, k)
            if c + 1 < len(chains):
                for ii in tiles_of(k):
                    gate_tile(c + 1, ii)
        if c + 1 < len(chains):
            for ii in range(n_i1):
                act_tile(c + 1, ii)


def _peer_dense(h2, e1, cnt, e2, r2, u_bf, vt_bf):
    t, d = h2.shape
    n_e = u_bf.shape[0]
    n = e1.shape[0]
    tm, te = TM_PEER, TE_PEER
    n_chain = tm // TS_PEER
    tok = lambda i, e: (0, i)
    once = pl.Buffered(1)
    return pl.pallas_call(
        functools.partial(_peer_dense_kernel, te=te, ts=TS_PEER),
        grid=(t // tm, n_e // te),
        in_specs=[pl.BlockSpec((tm, d), lambda i, e: (i, 0), pipeline_mode=once),
                  pl.BlockSpec((n, tm), tok, pipeline_mode=once), pl.BlockSpec((n, tm), tok, pipeline_mode=once),
                  pl.BlockSpec((n, tm), tok, pipeline_mode=once), pl.BlockSpec((n, tm), tok, pipeline_mode=once),
                  pl.BlockSpec((te, d), lambda i, e: (e, 0)),
                  pl.BlockSpec((d, te), lambda i, e: (0, e))],
        out_specs=pl.BlockSpec((d, tm), tok, pipeline_mode=once),
        out_shape=jax.ShapeDtypeStruct((d, t), F32),
        scratch_shapes=[pltpu.VMEM((n_chain, te, TS_PEER), F32), pltpu.VMEM((n_chain, te, TS_PEER), BF16),
                        pltpu.VMEM((n_chain, te, TS_PEER), EXPERT_DT)],
        compiler_params=_params("arbitrary", "arbitrary"),
        name="peer_dense",
    )(h2, e1, cnt, e2, r2, u_bf, vt_bf)


def _table_t_kernel(x_ref, o_ref):
    o_ref[...] = x_ref[...].T.astype(EXPERT_DT)


def _table_t(x):
    rows, cols = x.shape
    tr = 512
    return pl.pallas_call(
        _table_t_kernel,
        grid=(rows // tr,),
        in_specs=[pl.BlockSpec((tr, cols), lambda i: (i, 0))],
        out_specs=pl.BlockSpec((cols, tr), lambda i: (0, i)),
        out_shape=jax.ShapeDtypeStruct((cols, rows), EXPERT_DT),
        compiler_params=_params("arbitrary"),
        name="table_t",
    )(x)


def _final_ln_kernel(x1_ref, ft_ref, g2_ref, lg_ref, lb_ref, o_ref):
    o_ref[...] = _layer_norm(DN_ALPHA * x1_ref[...] + g2_ref[0] * ft_ref[...].T) * lg_ref[...] + lb_ref[...]


def _final_ln(x1, ffn_t, mod3, lg, lb, seq):
    t, d = x1.shape
    tm = TM_FINAL
    per_b = seq // tm
    row = lambda i: (i, 0)
    return pl.pallas_call(
        _final_ln_kernel,
        grid=(t // tm,),
        in_specs=[pl.BlockSpec((tm, d), row), pl.BlockSpec((d, tm), lambda i: (0, i)),
                  pl.BlockSpec((1, 1, d), lambda i: ((i // per_b) * 6 + 5, 0, 0)),
                  _resident(lg.shape), _resident(lb.shape)],
        out_specs=pl.BlockSpec((tm, d), row),
        out_shape=jax.ShapeDtypeStruct((t, d), F32),
        compiler_params=_params("arbitrary"),
        name="final_ln",
    )(x1, ffn_t, mod3, lg, lb)


def _rope_tables(seq, rot_dim):
    half = rot_dim // 2
    pos = jnp.arange(seq, dtype=F32)
    inv = ROPE_THETA ** (-jnp.arange(half, dtype=F32) * 2.0 / rot_dim)
    ang = pos[:, None] * inv[None, :]
    cos, sin = jnp.cos(ang), jnp.sin(ang)
    pad = V7X_LANES - rot_dim
    ones = jnp.ones((seq, pad), F32)
    zeros = jnp.zeros((seq, pad), F32)
    zh = jnp.zeros((seq, half), F32)
    return (jnp.concatenate([cos, cos, ones], 1),
            jnp.concatenate([-sin, zh, zeros], 1),
            jnp.concatenate([zh, sin, zeros], 1))


def _layer(x2, mod3, seq, w_in, g_q_lat, g_kv_lat, w_uq, w_uk, w_uv, g_out_a, g_out_b, w_o,
           ln1_g, ln1_b, w_pq, sub_key_1, sub_key_2, u_table, v_table, ln2_g, ln2_b):
    d = x2.shape[1]
    w_bf = w_in.astype(BF16)
    w_main = jnp.concatenate([w_bf[:, :2 * A_WIDTH], w_bf[:, 3 * A_WIDTH:]], axis=1)
    w_main = jnp.pad(w_main, ((0, 0), (0, (-w_main.shape[1]) % V7X_LANES)))
    wvt = w_bf[:, 2 * A_WIDTH:3 * A_WIDTH].T
    pad_q = MLA_QK_PAD - MLA_NOPE - MLA_ROPE
    wuq_r = jnp.pad(w_uq.astype(BF16).reshape(MLA_RANK, B_HEADS, MLA_NOPE + MLA_ROPE),
                    ((0, 0), (0, 0), (0, pad_q))).reshape(MLA_RANK, B_HEADS * MLA_QK_PAD)
    tabs_a = _rope_tables(seq, A_ROT_DIM)
    tabs_b = _rope_tables(seq, MLA_ROPE)
    qa, ka, vat, qb, kb, vbt = _in_proj(
        x2, mod3, w_main, wvt, wuq_r, w_uk.astype(BF16), w_uv.astype(BF16).T,
        g_q_lat.reshape(1, -1), g_kv_lat.reshape(1, -1), tabs_a, tabs_b, seq)

    bias_mix, bias_causal = _bias_tables(TQ_ATTN, seq)
    oa = _attention(qa, ka, vat, bias_mix, seq=seq, heads=A_HEADS, ek=HEAD_DIM, bias_all_keys=True)
    ob = _attention(qb, kb, vbt, bias_causal, seq=seq, heads=B_HEADS, ek=MLA_QK_PAD, bias_all_keys=False)

    sk = jnp.stack([sub_key_1, sub_key_2]).astype(BF16)
    x1, h2, st = _post_attn(oa, ob, x2, mod3, g_out_a.reshape(1, -1), g_out_b.reshape(1, -1),
                            w_o.astype(BF16), ln1_g.reshape(1, -1), ln1_b.reshape(1, -1),
                            w_pq.astype(BF16), sk, seq)
    e1, cnt, e2, r2 = _peer_topk(st)
    ffn_t = _peer_dense(h2, e1, cnt, e2, r2, u_table.astype(EXPERT_DT), _table_t(v_table))
    return _final_ln(x1, ffn_t, mod3, ln2_g.reshape(1, d), ln2_b.reshape(1, d), seq)


def kernel(x, c, w_ada, b_ada, w_in, g_q_lat, g_kv_lat, w_uq, w_uk, w_uv, g_out_a, g_out_b, w_o, ln1_g, ln1_b,
           w_pq, sub_key_1, sub_key_2, u_table, v_table, ln2_g, ln2_b):
    bsz, seq, d = x.shape
    x2 = x.reshape(bsz * seq, d)
    for l in range(w_ada.shape[0]):
        mod3 = _ada_mod(c, w_ada[l], b_ada[l]).reshape(bsz * 6, 1, d)
        x2 = _layer(x2, mod3, seq, w_in[l], g_q_lat[l], g_kv_lat[l], w_uq[l], w_uk[l], w_uv[l],
                    g_out_a[l], g_out_b[l], w_o[l], ln1_g[l], ln1_b[l], w_pq[l], sub_key_1[l], sub_key_2[l],
                    u_table[l], v_table[l], ln2_g[l], ln2_b[l])
    return x2.reshape(bsz, seq, d)
```

```python
import functools
import math

import jax
import jax.numpy as jnp
from jax import lax
from jax.experimental import pallas as pl
from jax.experimental.pallas import tpu as pltpu

F32 = jnp.float32
BF16 = jnp.bfloat16
F8 = jnp.float8_e4m3fn
F8_MAX = 448.0

D_MODEL = 2048
HEAD_DIM = 128
A_HEADS = 8
A_WIDTH = A_HEADS * HEAD_DIM
A_ROT_DIM = 32
ROPE_THETA = 500000.0
B_HEADS = 8
MLA_RANK = 512
MLA_NOPE = 128
MLA_ROPE = 64
MLA_V = 128
B_WIDTH = B_HEADS * MLA_V
MLA_QK_PAD = 256
PEER_HEADS = 8
PEER_N_KEYS = 128
PEER_TOPK = 16
DN_ALPHA = 2.0 ** 0.25

V7X_LANES = 128
V7X_SUBLANES = 8
V7X_BF16_SUBLANES = 16
V7X_VMEM_LIMIT = 56 * 1024 * 1024

TM_PROJ = 512
TQ_ATTN = 512
TL_TOPK = 256
TM_PEER = 1024
TS_PEER = 512
TE_PEER = 1024
PIECE_PEER = 256
TM_FINAL = 512

NEG_BIG = -1e30
LOG2E = math.log2(math.e)

_NT = (((1,), (1,)), ((), ()))

_C_QA, _C_KA, _C_CQ, _C_CKV, _C_KR = 0, A_WIDTH, 2 * A_WIDTH, 2 * A_WIDTH + MLA_RANK, 2 * A_WIDTH + 2 * MLA_RANK


def _params(*sem):
    return pltpu.CompilerParams(dimension_semantics=sem, vmem_limit_bytes=V7X_VMEM_LIMIT)


def _resident(shape):
    nd = len(shape)
    return pl.BlockSpec(shape, lambda *_: (0,) * nd, pipeline_mode=pl.Buffered(1))


def _layer_norm(x, eps=1e-5):
    mu = jnp.mean(x, axis=-1, keepdims=True)
    xc = x - mu
    var = jnp.mean(xc * xc, axis=-1, keepdims=True)
    return xc * lax.rsqrt(var + eps)


def _rms_norm(x, g, eps=1e-6):
    return x * lax.rsqrt(jnp.mean(x * x, axis=-1, keepdims=True) + eps) * g


def _to_f8(x):
    big = jnp.asarray(F8_MAX, x.dtype)
    return lax.clamp(-big, x, big).astype(F8)


def _rope_lanes(x, cos, sin_lo, sin_hi, half):
    return (x * cos + pltpu.roll(x, V7X_LANES - half, 1) * sin_lo + pltpu.roll(x, half, 1) * sin_hi)


def _ada_kernel(c_ref, w_ref, b_ref, o_ref):
    c = c_ref[...]
    act = (c / (1.0 + jnp.exp(-c))).astype(BF16)
    o_ref[...] = jnp.dot(act, w_ref[...].astype(BF16), preferred_element_type=F32) + b_ref[...]


def _ada_mod(c, w_ada, b_ada):
    bsz, d = c.shape
    n = w_ada.shape[1]
    tn = 1024
    return pl.pallas_call(
        _ada_kernel,
        grid=(n // tn,),
        in_specs=[pl.BlockSpec((bsz, d), lambda j: (0, 0)),
                  pl.BlockSpec((d, tn), lambda j: (0, j)),
                  pl.BlockSpec((1, tn), lambda j: (0, j))],
        out_specs=pl.BlockSpec((bsz, tn), lambda j: (0, j)),
        out_shape=jax.ShapeDtypeStruct((bsz, n), F32),
        compiler_params=_params("arbitrary"),
        name="ada_mod",
    )(c, w_ada, b_ada.reshape(1, n))


def _in_proj_kernel(x_ref, sc_ref, sh_ref, w_ref, wvt_ref, wuq_ref, wuk_ref, wuvt_ref, gq_ref, gkv_ref,
                    ca_ref, sal_ref, sah_ref, cb_ref, sbl_ref, sbh_ref,
                    qa_ref, ka_ref, vat_ref, qb_ref, kb_ref, vbt_ref, *, scale_a, scale_b):
    h = (_layer_norm(x_ref[...]) * (1.0 + sc_ref[0]) + sh_ref[0]).astype(BF16)

    def proj(c0, width):
        return jnp.dot(h, w_ref[:, c0:c0 + width], preferred_element_type=F32)

    ca, sal, sah = ca_ref[...], sal_ref[...], sah_ref[...]
    cb, sbl, sbh = cb_ref[...], sbl_ref[...], sbh_ref[...]
    chunk = 512
    for c_base, out_ref, scale in ((_C_QA, qa_ref, scale_a), (_C_KA, ka_ref, None)):
        for cc in range(A_WIDTH // chunk):
            acc = proj(c_base + cc * chunk, chunk)
            for hh in range(chunk // HEAD_DIM):
                r = _rope_lanes(acc[:, hh * HEAD_DIM:(hh + 1) * HEAD_DIM], ca, sal, sah, A_ROT_DIM // 2)
                if scale is not None:
                    r = r * scale
                c0 = cc * chunk + hh * HEAD_DIM
                out_ref[:, c0:c0 + HEAD_DIM] = r.astype(BF16)
    vat_ref[...] = lax.dot_general(wvt_ref[...], h, _NT, preferred_element_type=F32).astype(BF16)
    cq = _rms_norm(proj(_C_CQ, MLA_RANK), gq_ref[...]).astype(BF16)
    ckv = _rms_norm(proj(_C_CKV, MLA_RANK), gkv_ref[...]).astype(BF16)
    kr = _rope_lanes(proj(_C_KR, V7X_LANES), cb, sbl, sbh, MLA_ROPE // 2).astype(BF16)
    for cc in range(B_HEADS * MLA_QK_PAD // chunk):
        qq = jnp.dot(cq, wuq_ref[:, cc * chunk:(cc + 1) * chunk], preferred_element_type=F32)
        for hh in range(chunk // MLA_QK_PAD):
            c0 = cc * chunk + hh * MLA_QK_PAD
            nope = qq[:, hh * MLA_QK_PAD:hh * MLA_QK_PAD + MLA_NOPE]
            rp = qq[:, hh * MLA_QK_PAD + MLA_NOPE:(hh + 1) * MLA_QK_PAD]
            qb_ref[:, c0:c0 + MLA_NOPE] = (nope * scale_b).astype(BF16)
            qb_ref[:, c0 + MLA_NOPE:c0 + MLA_QK_PAD] = (
                _rope_lanes(rp, cb, sbl, sbh, MLA_ROPE // 2) * scale_b).astype(BF16)
    for cc in range(B_WIDTH // chunk):
        kn = jnp.dot(ckv, wuk_ref[:, cc * chunk:(cc + 1) * chunk], preferred_element_type=F32)
        for hh in range(chunk // MLA_NOPE):
            head = cc * (chunk // MLA_NOPE) + hh
            kb_ref[:, head * MLA_QK_PAD:head * MLA_QK_PAD + MLA_NOPE] = (
                kn[:, hh * MLA_NOPE:(hh + 1) * MLA_NOPE].astype(BF16))
            kb_ref[:, head * MLA_QK_PAD + MLA_NOPE:(head + 1) * MLA_QK_PAD] = kr
    vbt_ref[...] = lax.dot_general(wuvt_ref[...], ckv, _NT, preferred_element_type=F32).astype(BF16)


def _in_proj(x2, mod3, w_main, wvt, wuq_r, wuk, wuvt, gq, gkv, tabs_a, tabs_b, seq):
    t, d = x2.shape
    tm = TM_PROJ
    per_b = seq // tm
    row = lambda i: (i, 0)
    col = lambda i: (0, i)
    tab = lambda i: (i % per_b, 0)
    mod_spec = lambda k: pl.BlockSpec((1, 1, d), lambda i: ((i // per_b) * 6 + k, 0, 0))
    qk_b = B_HEADS * MLA_QK_PAD
    out_specs = [pl.BlockSpec((tm, A_WIDTH), row), pl.BlockSpec((tm, A_WIDTH), row),
                 pl.BlockSpec((A_WIDTH, tm), col),
                 pl.BlockSpec((tm, qk_b), row), pl.BlockSpec((tm, qk_b), row),
                 pl.BlockSpec((B_WIDTH, tm), col)]
    out_shape = [jax.ShapeDtypeStruct(s, BF16) for s in
                 ((t, A_WIDTH), (t, A_WIDTH), (A_WIDTH, t), (t, qk_b), (t, qk_b), (B_WIDTH, t))]
    return pl.pallas_call(
        functools.partial(_in_proj_kernel, scale_a=HEAD_DIM ** -0.5 * LOG2E,
                          scale_b=(MLA_NOPE + MLA_ROPE) ** -0.5 * LOG2E),
        grid=(t // tm,),
        in_specs=[pl.BlockSpec((tm, d), row), mod_spec(1), mod_spec(0),
                  _resident(w_main.shape), _resident(wvt.shape), _resident(wuq_r.shape), _resident(wuk.shape),
                  _resident(wuvt.shape), _resident(gq.shape), _resident(gkv.shape)]
                 + [pl.BlockSpec((tm, V7X_LANES), tab)] * 6,
        out_specs=out_specs,
        out_shape=out_shape,
        compiler_params=_params("arbitrary"),
        name="in_proj",
    )(x2, mod3, mod3, w_main, wvt, wuq_r, wuk, wuvt, gq, gkv, *tabs_a, *tabs_b)


def _attn_kernel(q_ref, k_ref, vt_ref, bias_ref, o_ref, *, tq, bias_all_keys):
    seq = q_ref.shape[0]
    nq = seq // tq
    nb = bias_ref.shape[0]

    def key_parts(qi):
        n = (qi + 1) * tq
        if bias_all_keys:
            return [(0, n, True)]
        return ([(0, n - tq, False)] if qi > 0 else []) + [(n - tq, n, True)]

    def score_tiles(qi):
        q = q_ref[qi * tq:(qi + 1) * tq, :]
        tiles = []
        for k0, k1, biased in key_parts(qi):
            st = lax.dot_general(k_ref[k0:k1, :], q, _NT, preferred_element_type=F32)
            tiles.append(st + bias_ref[nb - (k1 - k0):nb, :] if biased else st)
        return tiles

    ahead = score_tiles(0)
    for qi in range(nq):
        parts, scores = key_parts(qi), ahead
        if qi + 1 < nq:
            ahead = score_tiles(qi + 1)
        m = jnp.max(scores[0], axis=0, keepdims=True)
        for st in scores[1:]:
            m = jnp.maximum(m, jnp.max(st, axis=0, keepdims=True))
        l = jnp.zeros_like(m)
        o_t = jnp.zeros((vt_ref.shape[0], tq), F32)
        for (k0, k1, _), st in zip(parts, scores):
            p = jnp.exp2(st - m)
            l = l + jnp.sum(p, axis=0, keepdims=True)
            o_t = o_t + jnp.dot(vt_ref[:, k0:k1], p.astype(BF16), preferred_element_type=F32)
        o_ref[qi * tq:(qi + 1) * tq, :] = (o_t * (1.0 / l)).T


def _attention(q, k, vt, bias, *, seq, heads, ek, bias_all_keys):
    t = q.shape[0]
    bsz = t // seq
    ev = vt.shape[0] // heads
    return pl.pallas_call(
        functools.partial(_attn_kernel, tq=TQ_ATTN, bias_all_keys=bias_all_keys),
        grid=(bsz, heads),
        in_specs=[pl.BlockSpec((seq, ek), lambda b, h: (b, h)),
                  pl.BlockSpec((seq, ek), lambda b, h: (b, h)),
                  pl.BlockSpec((ev, seq), lambda b, h: (h, b)),
                  _resident(bias.shape)],
        out_specs=pl.BlockSpec((seq, ev), lambda b, h: (b, h)),
        out_shape=jax.ShapeDtypeStruct((t, heads * ev), F32),
        compiler_params=_params("arbitrary", "arbitrary"),
        name="attention",
    )(q, k, vt, bias)


def _bias_tables(tq, seq):
    d = ((seq - tq) + jnp.arange(tq, dtype=jnp.int32)[None, :] - jnp.arange(seq, dtype=jnp.int32)[:, None])
    ok = d >= 0
    mult = ((d <= 128).astype(F32) + ((d % 4 == 0) & (d <= 512)).astype(F32)
            + ((d % 16 == 0) & (d <= 2048)).astype(F32))
    mix = jnp.where(ok & (mult > 0), jnp.log2(jnp.maximum(mult, 1.0)), NEG_BIG).astype(F32)
    causal = jnp.where(ok[seq - tq:], 0.0, NEG_BIG).astype(F32)
    return mix, causal


def _post_attn_kernel(oa_ref, ob_ref, x_ref, g1_ref, sc_ref, sh_ref, ga_ref, gb_ref, wo_ref, l1g_ref, l1b_ref,
                      wpq_ref, sk_ref, x1_ref, h2_ref, st_ref):
    na = _rms_norm(oa_ref[...], ga_ref[...]).astype(BF16)
    nb = _rms_norm(ob_ref[...], gb_ref[...]).astype(BF16)
    mix = (jnp.dot(na, wo_ref[0:A_WIDTH, :], preferred_element_type=F32)
           + jnp.dot(nb, wo_ref[A_WIDTH:A_WIDTH + B_WIDTH, :], preferred_element_type=F32))
    x1 = _layer_norm(DN_ALPHA * x_ref[...] + g1_ref[0] * mix) * l1g_ref[...] + l1b_ref[...]
    x1_ref[...] = x1
    h2_f32 = _layer_norm(x1) * (1.0 + sc_ref[0]) + sh_ref[0]
    h2_ref[...] = _to_f8(h2_f32)
    q = jnp.dot(h2_f32.astype(BF16), wpq_ref[...], preferred_element_type=F32).astype(BF16)
    for hh in range(PEER_HEADS):
        for half in range(2):
            r0 = (hh * 2 + half) * PEER_N_KEYS
            st_ref[r0:r0 + PEER_N_KEYS, :] = lax.dot_general(
                sk_ref[half], q[:, r0:r0 + PEER_N_KEYS], _NT, preferred_element_type=F32)


def _post_attn(oa, ob, x2, mod3, ga, gb, wo, l1g, l1b, wpq, sk, seq):
    t, d = x2.shape
    tm = TM_PROJ
    per_b = seq // tm
    row = lambda i: (i, 0)
    mod_spec = lambda k: pl.BlockSpec((1, 1, d), lambda i: ((i // per_b) * 6 + k, 0, 0))
    n_sc = 2 * PEER_HEADS * PEER_N_KEYS
    return pl.pallas_call(
        _post_attn_kernel,
        grid=(t // tm,),
        in_specs=[pl.BlockSpec((tm, A_WIDTH), row), pl.BlockSpec((tm, B_WIDTH), row), pl.BlockSpec((tm, d), row),
                  mod_spec(2), mod_spec(4), mod_spec(3),
                  _resident(ga.shape), _resident(gb.shape), _resident(wo.shape),
                  _resident(l1g.shape), _resident(l1b.shape), _resident(wpq.shape), _resident(sk.shape)],
        out_specs=[pl.BlockSpec((tm, d), row), pl.BlockSpec((tm, d), row),
                   pl.BlockSpec((n_sc, tm), lambda i: (0, i))],
        out_shape=[jax.ShapeDtypeStruct((t, d), F32), jax.ShapeDtypeStruct((t, d), F8),
                   jax.ShapeDtypeStruct((n_sc, t), F32)],
        compiler_params=_params("arbitrary"),
        name="post_attn",
    )(oa, ob, x2, mod3, mod3, mod3, ga, gb, wo, l1g, l1b, wpq, sk)


_GROUPS = PEER_N_KEYS // V7X_SUBLANES


def _oddeven_merge_sort_pairs(n):
    pairs = []
    p = 1
    while p < n:
        k = p
        while k >= 1:
            for j in range(k % p, n - k, 2 * k):
                for i in range(min(k, n - j - k)):
                    if (i + j) // (2 * p) == (i + j + k) // (2 * p):
                        pairs.append((i + j, i + j + k))
            k //= 2
        p *= 2
    return pairs


_SORT16 = _oddeven_merge_sort_pairs(_GROUPS)


def _sublane_max(x):
    for shift in (4, 2, 1):
        x = jnp.maximum(x, pltpu.roll(x, shift, 0))
    return x


def _sorted_top16(groups):
    c = list(groups)
    for i, j in _SORT16:
        c[i], c[j] = jnp.maximum(c[i], c[j]), jnp.minimum(c[i], c[j])
    out = []
    for r in range(PEER_TOPK):
        m = _sublane_max(c[0])
        out.append(m)
        if r + 1 < PEER_TOPK:
            hit = c[0] == m
            c = [jnp.where(hit, c[k + 1], c[k]) for k in range(len(c) - 1)]
    return out


def _count_leading(pred, v):
    p8 = pred(v[7])
    p4 = pred(jnp.where(p8, v[11], v[3]))
    p2 = pred(jnp.where(p8, jnp.where(p4, v[13], v[9]), jnp.where(p4, v[5], v[1])))
    hi = jnp.where(p4, jnp.where(p2, v[14], v[12]), jnp.where(p2, v[10], v[8]))
    lo = jnp.where(p4, jnp.where(p2, v[6], v[4]), jnp.where(p2, v[2], v[0]))
    p1 = pred(jnp.where(p8, hi, lo))
    p16 = pred(v[15])
    one, zero = jnp.float32(1.0), jnp.float32(0.0)
    return (jnp.where(p8, 8.0, zero) + jnp.where(p4, 4.0, zero) + jnp.where(p2, 2.0, zero)
            + jnp.where(p1, one, zero) + jnp.where(p16, one, zero))


def _peer_topk_kernel(st_ref, e1_ref, cnt_ref, e2_ref, r2_ref):
    tl = st_ref.shape[1]
    sub = lax.broadcasted_iota(jnp.int32, (V7X_SUBLANES, tl), 0)

    def by_sublane(rows):
        out = rows[0]
        for j in range(1, V7X_SUBLANES):
            out = jnp.where(sub == j, rows[j], out)
        return out

    for hh in range(PEER_HEADS):
        r0 = hh * 2 * PEER_N_KEYS
        g1 = [st_ref[r0 + V7X_SUBLANES * k:r0 + V7X_SUBLANES * (k + 1), :] for k in range(_GROUPS)]
        g2 = [st_ref[r0 + PEER_N_KEYS + V7X_SUBLANES * k:r0 + PEER_N_KEYS + V7X_SUBLANES * (k + 1), :]
              for k in range(_GROUPS)]
        v1 = _sorted_top16(g1)
        v2 = _sorted_top16(g2)
        v2_lo, v2_hi, v1_hi = by_sublane(v2[:8]), by_sublane(v2[8:]), by_sublane(v1[8:])
        cand = [v1[0] + v2_lo, v1[0] + v2_hi, v1[1] + v2_lo]
        for a in range(2, 8):
            cand.append(jnp.where(sub < PEER_TOPK // (a + 1), v1[a] + v2_lo, -jnp.inf))
        cand.append(v1_hi + v2[0])
        cmax = v1[0] + v2[0]
        z = jnp.zeros_like(cmax)
        tau = cmax
        for r in range(PEER_TOPK):
            tau = cand[0]
            for blk in cand[1:]:
                tau = jnp.maximum(tau, blk)
            tau = _sublane_max(tau)
            z = z + jnp.exp(tau - cmax)
            if r + 1 < PEER_TOPK:
                cand = [jnp.where(blk == tau, -jnp.inf, blk) for blk in cand]
        inv_z = 1.0 / z
        for k in range(_GROUPS):
            rows = slice(hh * PEER_N_KEYS + V7X_SUBLANES * k, hh * PEER_N_KEYS + V7X_SUBLANES * (k + 1))
            x1 = g1[k]
            e1_ref[rows, :] = jnp.exp(x1 - v1[0]) * inv_z
            cnt_ref[rows, :] = _count_leading(lambda vb: x1 + vb >= tau, v2)
        for k in range(0, _GROUPS, 2):
            rows = slice(hh * PEER_N_KEYS + V7X_SUBLANES * k, hh * PEER_N_KEYS + V7X_SUBLANES * (k + 2))
            e2_ref[rows, :] = jnp.concatenate(
                [jnp.exp(g2[k + j] - v2[0]) for j in range(2)], axis=0).astype(BF16)
            r2_ref[rows, :] = jnp.concatenate(
                [_count_leading(lambda vb, x=g2[k + j]: vb > x, v2) for j in range(2)], axis=0).astype(BF16)


def _peer_topk(st):
    rows, t = st.shape
    tl = TL_TOPK
    n = PEER_HEADS * PEER_N_KEYS
    col = lambda i: (0, i)
    return pl.pallas_call(
        _peer_topk_kernel,
        grid=(t // tl,),
        in_specs=[pl.BlockSpec((rows, tl), col)],
        out_specs=[pl.BlockSpec((n, tl), col)] * 4,
        out_shape=[jax.ShapeDtypeStruct((n, t), F32), jax.ShapeDtypeStruct((n, t), F32),
                   jax.ShapeDtypeStruct((n, t), BF16), jax.ShapeDtypeStruct((n, t), BF16)],
        compiler_params=_params("arbitrary"),
        name="peer_topk",
    )(st)


def _gelu(x):
    half = 0.5 * x
    return half + half * lax.erf(x * (2.0 ** -0.5))


def _row_bf16(ref, row, cols, rows_out):
    r = jnp.broadcast_to(ref[pl.ds(row, 1), cols], (V7X_BF16_SUBLANES, cols.stop - cols.start)).astype(BF16)
    return jnp.concatenate([r] * (rows_out // V7X_BF16_SUBLANES), axis=0)


def _peer_dense_kernel(h2_ref, e1_ref, cnt_ref, e2_ref, r2_ref, u_ref, vt_ref, o_ref, a_sc, w_sc, g_sc, *, te, ts):
    e = pl.program_id(1)
    n_i1 = te // PEER_N_KEYS
    tm = h2_ref.shape[0]

    @pl.when(e == 0)
    def _():
        o_ref[...] = jnp.zeros(o_ref.shape, F32)

    chains = [slice(s * ts, (s + 1) * ts) for s in range(tm // ts)]
    d_model = vt_ref.shape[0]
    n_piece = te // PIECE_PEER
    per_piece = PIECE_PEER // PEER_N_KEYS
    d_piece = d_model // n_piece

    def first_piece(c, k):
        rows = slice(k * PIECE_PEER, (k + 1) * PIECE_PEER)
        a_sc[c, rows, :] = lax.dot_general(u_ref[rows, :], h2_ref[chains[c], :], _NT,
                                           preferred_element_type=F32)

    def gate_tile(c, ii):
        i1 = e * n_i1 + ii
        w = jnp.zeros((PEER_N_KEYS, ts), BF16)
        for hh in range(PEER_HEADS):
            rows = slice(hh * PEER_N_KEYS, (hh + 1) * PEER_N_KEYS)
            e1 = _row_bf16(e1_ref, hh * PEER_N_KEYS + i1, chains[c], PEER_N_KEYS)
            cnt = _row_bf16(cnt_ref, hh * PEER_N_KEYS + i1, chains[c], PEER_N_KEYS)
            w = w + jnp.where(r2_ref[rows, chains[c]] < cnt, e2_ref[rows, chains[c]], jnp.zeros_like(w)) * e1
        w_sc[c, ii * PEER_N_KEYS:(ii + 1) * PEER_N_KEYS, :] = w

    def act_tile(c, ii):
        rows = slice(ii * PEER_N_KEYS, (ii + 1) * PEER_N_KEYS)
        g_sc[c, rows, :] = _to_f8(_gelu(a_sc[c, rows, :].astype(BF16)) * w_sc[c, rows, :])

    def second_piece(c, k):
        rows = slice(k * d_piece, (k + 1) * d_piece)
        o_ref[rows, chains[c]] += jnp.dot(vt_ref[rows, :], g_sc[c], preferred_element_type=F32)

    def tiles_of(k):
        return range(k * per_piece, (k + 1) * per_piece)

    for c in range(len(chains)):
        for k in range(n_piece):
            first_piece(c, k)
            for ii in tiles_of(k):
                if c == 0:
                    gate_tile(0, ii)
                if c == 1 or len(chains) == 1:
                    act_tile(0, ii)
    for c in range(len(chains)):
        for k in range(n_piece):
            second_piece(c, k)
            if c + 1 < len(chains):
                for ii in tiles_of(k):
                    gate_tile(c + 1, ii)
        if c + 1 < len(chains):
            for ii in range(n_i1):
                act_tile(c + 1, ii)


def _peer_dense(h2, e1, cnt, e2, r2, u_bf, vt_bf):
    t, d = h2.shape
    n_e = u_bf.shape[0]
    n = e1.shape[0]
    tm, te = TM_PEER, TE_PEER
    n_chain = tm // TS_PEER
    tok = lambda i, e: (0, i)
    once = pl.Buffered(1)
    return pl.pallas_call(
        functools.partial(_peer_dense_kernel, te=te, ts=TS_PEER),
        grid=(t // tm, n_e // te),
        in_specs=[pl.BlockSpec((tm, d), lambda i, e: (i, 0)),
                  pl.BlockSpec((n, tm), tok), pl.BlockSpec((n, tm), tok),
                  pl.BlockSpec((n, tm), tok), pl.BlockSpec((n, tm), tok),
                  pl.BlockSpec((te, d), lambda i, e: (e, 0)),
                  pl.BlockSpec((d, te), lambda i, e: (0, e))],
        out_specs=pl.BlockSpec((d, tm), tok, pipeline_mode=once),
        out_shape=jax.ShapeDtypeStruct((d, t), F32),
        scratch_shapes=[pltpu.VMEM((n_chain, te, TS_PEER), F32), pltpu.VMEM((n_chain, te, TS_PEER), BF16),
                        pltpu.VMEM((n_chain, te, TS_PEER), F8)],
        compiler_params=_params("arbitrary", "arbitrary"),
        name="peer_dense",
    )(h2, e1, cnt, e2, r2, u_bf, vt_bf)


def _table_t_kernel(x_ref, o_ref):
    o_ref[...] = _to_f8(x_ref[...].T)


def _table_t(x):
    rows, cols = x.shape
    tr = 512
    return pl.pallas_call(
        _table_t_kernel,
        grid=(rows // tr,),
        in_specs=[pl.BlockSpec((tr, cols), lambda i: (i, 0))],
        out_specs=pl.BlockSpec((cols, tr), lambda i: (0, i)),
        out_shape=jax.ShapeDtypeStruct((cols, rows), F8),
        compiler_params=_params("arbitrary"),
        name="table_t",
    )(x)


def _final_ln_kernel(x1_ref, ft_ref, g2_ref, lg_ref, lb_ref, o_ref):
    o_ref[...] = _layer_norm(DN_ALPHA * x1_ref[...] + g2_ref[0] * ft_ref[...].T) * lg_ref[...] + lb_ref[...]


def _final_ln(x1, ffn_t, mod3, lg, lb, seq):
    t, d = x1.shape
    tm = TM_FINAL
    per_b = seq // tm
    row = lambda i: (i, 0)
    return pl.pallas_call(
        _final_ln_kernel,
        grid=(t // tm,),
        in_specs=[pl.BlockSpec((tm, d), row), pl.BlockSpec((d, tm), lambda i: (0, i)),
                  pl.BlockSpec((1, 1, d), lambda i: ((i // per_b) * 6 + 5, 0, 0)),
                  _resident(lg.shape), _resident(lb.shape)],
        out_specs=pl.BlockSpec((tm, d), row),
        out_shape=jax.ShapeDtypeStruct((t, d), F32),
        compiler_params=_params("arbitrary"),
        name="final_ln",
    )(x1, ffn_t, mod3, lg, lb)


def _rope_tables(seq, rot_dim):
    half = rot_dim // 2
    pos = jnp.arange(seq, dtype=F32)
    inv = ROPE_THETA ** (-jnp.arange(half, dtype=F32) * 2.0 / rot_dim)
    ang = pos[:, None] * inv[None, :]
    cos, sin = jnp.cos(ang), jnp.sin(ang)
    pad = V7X_LANES - rot_dim
    ones = jnp.ones((seq, pad), F32)
    zeros = jnp.zeros((seq, pad), F32)
    zh = jnp.zeros((seq, half), F32)
    return (jnp.concatenate([cos, cos, ones], 1),
            jnp.concatenate([-sin, zh, zeros], 1),
            jnp.concatenate([zh, sin, zeros], 1))


def _layer(x2, mod3, seq, w_in, g_q_lat, g_kv_lat, w_uq, w_uk, w_uv, g_out_a, g_out_b, w_o,
           ln1_g, ln1_b, w_pq, sub_key_1, sub_key_2, u_table, v_table, ln2_g, ln2_b):
    d = x2.shape[1]
    w_bf = w_in.astype(BF16)
    w_main = jnp.concatenate([w_bf[:, :2 * A_WIDTH], w_bf[:, 3 * A_WIDTH:]], axis=1)
    w_main = jnp.pad(w_main, ((0, 0), (0, (-w_main.shape[1]) % V7X_LANES)))
    wvt = w_bf[:, 2 * A_WIDTH:3 * A_WIDTH].T
    pad_q = MLA_QK_PAD - MLA_NOPE - MLA_ROPE
    wuq_r = jnp.pad(w_uq.astype(BF16).reshape(MLA_RANK, B_HEADS, MLA_NOPE + MLA_ROPE),
                    ((0, 0), (0, 0), (0, pad_q))).reshape(MLA_RANK, B_HEADS * MLA_QK_PAD)
    tabs_a = _rope_tables(seq, A_ROT_DIM)
    tabs_b = _rope_tables(seq, MLA_ROPE)
    qa, ka, vat, qb, kb, vbt = _in_proj(
        x2, mod3, w_main, wvt, wuq_r, w_uk.astype(BF16), w_uv.astype(BF16).T,
        g_q_lat.reshape(1, -1), g_kv_lat.reshape(1, -1), tabs_a, tabs_b, seq)

    bias_mix, bias_causal = _bias_tables(TQ_ATTN, seq)
    oa = _attention(qa, ka, vat, bias_mix, seq=seq, heads=A_HEADS, ek=HEAD_DIM, bias_all_keys=True)
    ob = _attention(qb, kb, vbt, bias_causal, seq=seq, heads=B_HEADS, ek=MLA_QK_PAD, bias_all_keys=False)

    sk = jnp.stack([sub_key_1, sub_key_2]).astype(BF16)
    x1, h2, st = _post_attn(oa, ob, x2, mod3, g_out_a.reshape(1, -1), g_out_b.reshape(1, -1),
                            w_o.astype(BF16), ln1_g.reshape(1, -1), ln1_b.reshape(1, -1),
                            w_pq.astype(BF16), sk, seq)
    e1, cnt, e2, r2 = _peer_topk(st)
    ffn_t = _peer_dense(h2, e1, cnt, e2, r2, _to_f8(u_table), _table_t(v_table))
    return _final_ln(x1, ffn_t, mod3, ln2_g.reshape(1, d), ln2_b.reshape(1, d), seq)


def kernel(x, c, w_ada, b_ada, w_in, g_q_lat, g_kv_lat, w_uq, w_uk, w_uv, g_out_a, g_out_b, w_o, ln1_g, ln1_b,
           w_pq, sub_key_1, sub_key_2, u_table, v_table, ln2_g, ln2_b):
    bsz, seq, d = x.shape
    x2 = x.reshape(bsz * seq, d)
    for l in range(w_ada.shape[0]):
        mod3 = _ada_mod(c, w_ada[l], b_ada[l]).reshape(bsz * 6, 1, d)
        x2 = _layer(x2, mod3, seq, w_in[l], g_q_lat[l], g_kv_lat[l], w_uq[l], w_uk[l], w_uv[l],
                    g_out_a[l], g_out_b[l], w_o[l], ln1_g[l], ln1_b[l], w_pq[l], sub_key_1[l], sub_key_2[l],
                    u_table[l], v_table[l], ln2_g[l], ln2_b[l])
    return x2.reshape(bsz, seq, d)
```
